```python
import math
import jax, jax.numpy as jnp
from jax import lax
import numpy as np


D_MODEL = 1024
BATCH = 8
SEQ = 8192
DEPTH = 2

SSM_WIDTH = D_MODEL // 2
SSM_GROUP = 16
SSM_GROUPS = SSM_WIDTH // SSM_GROUP
SSM_STATE = 64
DT_MIN = 1e-3
DT_MAX = 1e-1
N_Q_HEADS = 8
N_KV_HEADS = 2
HEAD_DIM = 64
GQA_GROUP = N_Q_HEADS // N_KV_HEADS
CMP_BLOCK = 32
CMP_STRIDE = 16
CMP_HIDDEN = 256
SEL_BLOCK = 64
N_SEL = 16
WINDOW = 512
Q_BLOCK = 128
ROPE_THETA = 10000.0
N_NSA_BRANCH = 3
D_FF = 2816
CONV_WIDTH = 3
RMS_EPS = 1e-6
NEG_INF = -1e30
FORCE_BONUS = 1e4
Q_WIDTH = N_Q_HEADS * HEAD_DIM
KV_WIDTH = N_KV_HEADS * HEAD_DIM
IN_SPLITS = (SSM_WIDTH, Q_WIDTH, KV_WIDTH, KV_WIDTH, KV_WIDTH, KV_WIDTH, KV_WIDTH, KV_WIDTH, N_Q_HEADS * N_NSA_BRANCH, D_MODEL, D_MODEL)
IN_WIDTH = sum(IN_SPLITS)

kernel_name = 'hybrid_s5_nsa_convffn'


def rmsnorm(x, g):
    xf = x.astype(jnp.float32)
    y = xf * lax.rsqrt(jnp.mean(xf * xf, axis=-1, keepdims=True) + RMS_EPS)
    return (y * g.astype(jnp.float32)).astype(x.dtype)


def rope_tables(seq):
    inv_freq = ROPE_THETA ** (-jnp.arange(0, HEAD_DIM, 2, dtype=jnp.float32) / HEAD_DIM)
    ang = jnp.arange(seq, dtype=jnp.float32)[:, None] * inv_freq[None, :]
    return jnp.cos(ang), jnp.sin(ang)


def apply_rope(x, cos, sin):
    x1, x2 = jnp.split(x, 2, axis=-1)
    c = cos[None, :, None, :]
    s = sin[None, :, None, :]
    return jnp.concatenate([x1 * c - x2 * s, x1 * s + x2 * c], axis=-1).astype(x.dtype)


def masked_softmax(s, mask):
    p = jax.nn.softmax(jnp.where(mask, s, NEG_INF), axis=-1)
    return jnp.where(mask, p, 0.0)


def complex_affine_combine(e1, e2):
    a1r, a1i, b1r, b1i = e1
    a2r, a2i, b2r, b2i = e2
    return (a1r * a2r - a1i * a2i,
            a1r * a2i + a1i * a2r,
            a2r * b1r - a2i * b1i + b2r,
            a2r * b1i + a2i * b1r + b2i)


def s5_mixer(u, a_re, a_im, log_dt, b_re, b_im, c_re, c_im, d, w_glu):
    bsz, seq, _ = u.shape
    f32 = jnp.float32
    uf = u.astype(f32).reshape(bsz, seq, SSM_GROUPS, SSM_GROUP)
    lr = a_re.astype(f32)
    li = a_im.astype(f32)
    dt = jnp.exp(log_dt.astype(f32))[:, None]
    mag = jnp.exp(lr * dt)
    ab_re = mag * jnp.cos(li * dt)
    ab_im = mag * jnp.sin(li * dt)
    den = lr * lr + li * li
    f_re = ((ab_re - 1.0) * lr + ab_im * li) / den
    f_im = (ab_im * lr - (ab_re - 1.0) * li) / den
    bu_re = jnp.einsum('bsgh,gph->bsgp', uf, b_re.astype(f32))
    bu_im = jnp.einsum('bsgh,gph->bsgp', uf, b_im.astype(f32))
    bb_re = f_re * bu_re - f_im * bu_im
    bb_im = f_re * bu_im + f_im * bu_re
    a_seq_re = jnp.broadcast_to(ab_re, (1, seq, SSM_GROUPS, SSM_STATE))
    a_seq_im = jnp.broadcast_to(ab_im, (1, seq, SSM_GROUPS, SSM_STATE))
    _, _, x_re, x_im = lax.associative_scan(complex_affine_combine, (a_seq_re, a_seq_im, bb_re, bb_im), axis=1)
    y = (jnp.einsum('bsgp,ghp->bsgh', x_re, c_re.astype(f32))
         - jnp.einsum('bsgp,ghp->bsgh', x_im, c_im.astype(f32))
         + d.astype(f32).reshape(SSM_GROUPS, SSM_GROUP) * uf)
    y = jax.nn.gelu(y.reshape(bsz, seq, SSM_WIDTH)).astype(u.dtype)
    z_val, z_gate = jnp.split(y @ w_glu, 2, axis=-1)
    return z_val * jax.nn.sigmoid(z_gate)


def compress_blocks(k, pe, w1, b1, w2):
    bsz, seq = k.shape[:2]
    n_cmp = (seq - CMP_BLOCK) // CMP_STRIDE + 1
    idx = jnp.arange(n_cmp)[:, None] * CMP_STRIDE + jnp.arange(CMP_BLOCK)[None, :]
    blocks = k[:, idx] + pe[None, None, :, None, :]
    blocks = blocks.transpose(0, 1, 3, 2, 4).reshape(bsz, n_cmp, N_KV_HEADS, CMP_BLOCK * HEAD_DIM)
    return jax.nn.gelu(blocks @ w1 + b1) @ w2


def nsa_mixer(q, k_cmp, v_cmp, k_sel, v_sel, k_win, v_win, gate_logits,
              pe_k, w1_k, b1_k, w2_k, pe_v, w1_v, b1_v, w2_v):
    bsz, seq = q.shape[:2]
    f32 = jnp.float32
    scale = HEAD_DIM ** -0.5
    n_blk = seq // SEL_BLOCK
    n_top = min(N_SEL, n_blk)
    ratio = SEL_BLOCK // CMP_STRIDE
    span = ratio + CMP_BLOCK // CMP_STRIDE - 1
    q = q.reshape(bsz, seq, N_KV_HEADS, GQA_GROUP, HEAD_DIM)
    gates = jax.nn.sigmoid(gate_logits).reshape(bsz, seq, N_KV_HEADS, GQA_GROUP, N_NSA_BRANCH)
    kc = compress_blocks(k_cmp, pe_k, w1_k, b1_k, w2_k)
    vc = compress_blocks(v_cmp, pe_v, w1_v, b1_v, w2_v)
    n_cmp = kc.shape[1]
    cmp_end = jnp.arange(n_cmp) * CMP_STRIDE + CMP_BLOCK - 1
    pad_l = CMP_BLOCK // CMP_STRIDE - 1
    pad_r = ratio * (n_blk - 1) + span - n_cmp - pad_l
    kb = k_sel.reshape(bsz, n_blk, SEL_BLOCK, N_KV_HEADS, HEAD_DIM).transpose(0, 3, 1, 2, 4)
    vb = v_sel.reshape(bsz, n_blk, SEL_BLOCK, N_KV_HEADS, HEAD_DIM).transpose(0, 3, 1, 2, 4)
    kw_pad = jnp.pad(k_win, ((0, 0), (WINDOW, 0), (0, 0), (0, 0)))
    vw_pad = jnp.pad(v_win, ((0, 0), (WINDOW, 0), (0, 0), (0, 0)))
    blk_ids = jnp.arange(n_blk)
    gather_blocks = jax.vmap(jax.vmap(lambda blocks, ids: blocks[ids]))

    def query_block(qi):
        s0 = qi * Q_BLOCK
        qb = lax.dynamic_slice_in_dim(q, s0, Q_BLOCK, axis=1)
        gb = lax.dynamic_slice_in_dim(gates, s0, Q_BLOCK, axis=1)
        t = s0 + jnp.arange(Q_BLOCK)
        s_c = jnp.einsum('bqhgd,bnhd->bhgqn', qb, kc).astype(f32) * scale
        p_c = masked_softmax(s_c, cmp_end[None, :] <= t[:, None])
        o_cmp = jnp.einsum('bhgqn,bnhd->bqhgd', p_c.astype(vc.dtype), vc)
        p_grp = jnp.pad(p_c.sum(axis=2), ((0, 0), (0, 0), (0, 0), (pad_l, pad_r)))
        blk_score = sum(p_grp[..., o: o + ratio * (n_blk - 1) + 1: ratio] for o in range(span))
        cur = t // SEL_BLOCK
        forced = (blk_ids[None, :] == 0) | (blk_ids[None, :] == cur[:, None]) | (blk_ids[None, :] == cur[:, None] - 1)
        causal_b = blk_ids[None, :] * SEL_BLOCK <= t[:, None]
        blk_score = jnp.where(causal_b, blk_score + jnp.where(forced, FORCE_BONUS, 0.0), NEG_INF)
        _, top_idx = lax.top_k(blk_score, n_top)
        k_g = gather_blocks(kb, top_idx)
        v_g = gather_blocks(vb, top_idx)
        k_pos = top_idx[..., None] * SEL_BLOCK + jnp.arange(SEL_BLOCK)
        sel_mask = (k_pos <= t[None, None, :, None, None]).reshape(bsz, N_KV_HEADS, 1, Q_BLOCK, n_top * SEL_BLOCK)
        s_s = jnp.einsum('bqhgd,bhqnkd->bhgqnk', qb, k_g).astype(f32) * scale
        p_s = masked_softmax(s_s.reshape(bsz, N_KV_HEADS, GQA_GROUP, Q_BLOCK, n_top * SEL_BLOCK), sel_mask)
        o_sel = jnp.einsum('bhgqnk,bhqnkd->bqhgd', p_s.reshape(s_s.shape).astype(v_g.dtype), v_g)
        k_w = lax.dynamic_slice_in_dim(kw_pad, s0, WINDOW + Q_BLOCK, axis=1)
        v_w = lax.dynamic_slice_in_dim(vw_pad, s0, WINDOW + Q_BLOCK, axis=1)
        w_pos = s0 - WINDOW + jnp.arange(WINDOW + Q_BLOCK)
        win_mask = (w_pos[None, :] <= t[:, None]) & (w_pos[None, :] > t[:, None] - WINDOW) & (w_pos[None, :] >= 0)
        s_w = jnp.einsum('bqhgd,bkhd->bhgqk', qb, k_w).astype(f32) * scale
        p_w = masked_softmax(s_w, win_mask)
        o_win = jnp.einsum('bhgqk,bkhd->bqhgd', p_w.astype(v_w.dtype), v_w)
        return gb[..., 0:1] * o_cmp + gb[..., 1:2] * o_sel + gb[..., 2:3] * o_win

    out = lax.map(query_block, jnp.arange(seq // Q_BLOCK))
    return jnp.moveaxis(out, 0, 1).reshape(bsz, seq, Q_WIDTH)


def hybrid_mixer(h, cos, sin, w_in, a_re, a_im, log_dt, b_re, b_im, c_re, c_im, d, w_glu,
                 pe_k, w1_k, b1_k, w2_k, pe_v, w1_v, b1_v, w2_v, w_branch_ssm, w_branch_nsa, w_out):
    bsz, seq, _ = h.shape
    split_at = np.cumsum(IN_SPLITS)[:-1].tolist()
    (u, q, k_cmp, v_cmp, k_sel, v_sel, k_win, v_win,
     nsa_gate, g_ssm, g_nsa) = jnp.split(h @ w_in, split_at, axis=-1)
    y_ssm = s5_mixer(u, a_re, a_im, log_dt, b_re, b_im, c_re, c_im, d, w_glu) @ w_branch_ssm
    heads = lambda z, n: z.reshape(bsz, seq, n, HEAD_DIM)
    q = apply_rope(heads(q, N_Q_HEADS), cos, sin)
    k_cmp = apply_rope(heads(k_cmp, N_KV_HEADS), cos, sin)
    k_sel = apply_rope(heads(k_sel, N_KV_HEADS), cos, sin)
    k_win = apply_rope(heads(k_win, N_KV_HEADS), cos, sin)
    y_nsa = nsa_mixer(q, k_cmp, heads(v_cmp, N_KV_HEADS), k_sel, heads(v_sel, N_KV_HEADS),
                      k_win, heads(v_win, N_KV_HEADS), nsa_gate,
                      pe_k, w1_k, b1_k, w2_k, pe_v, w1_v, b1_v, w2_v) @ w_branch_nsa
    merged = jax.nn.sigmoid(g_ssm) * y_ssm + jax.nn.sigmoid(g_nsa) * y_nsa
    return merged @ w_out


def conv_ffn(h, w_in, conv_w, conv_b, w_out):
    a, b = jnp.split(h @ w_in, 2, axis=-1)
    a = lax.conv_general_dilated(a, conv_w[:, None, :], window_strides=(1,),
                                 padding=[(CONV_WIDTH - 1, 0)],
                                 dimension_numbers=('NWC', 'WIO', 'NWC'),
                                 feature_group_count=D_FF) + conv_b
    return (jax.nn.gelu(a) * b) @ w_out


def setup_inputs(seed: int = 0) -> dict:
    key = jax.random.key(seed)
    ks = jax.random.split(key, 29)
    f32 = jnp.float32

    def nrm(k, shape, scale):
        return jax.random.normal(k, shape, f32) * scale

    L, G, P, H = DEPTH, SSM_GROUPS, SSM_STATE, SSM_GROUP
    a_im0 = math.pi * jnp.arange(P, dtype=f32)
    cmp_in = CMP_BLOCK * HEAD_DIM
    return {
        'x': nrm(ks[0], (BATCH, SEQ, D_MODEL), 1.0),
        'norm_mix': 1.0 + nrm(ks[1], (L, D_MODEL), 0.01),
        'w_in': nrm(ks[2], (L, D_MODEL, IN_WIDTH), D_MODEL ** -0.5),
        'ssm_a_re': -0.5 + nrm(ks[3], (L, G, P), 0.01),
        'ssm_a_im': a_im0 + nrm(ks[4], (L, G, P), 0.01),
        'ssm_log_dt': jax.random.uniform(ks[5], (L, G), f32, math.log(DT_MIN), math.log(DT_MAX)),
        'ssm_b_re': nrm(ks[6], (L, G, P, H), (2 * H) ** -0.5),
        'ssm_b_im': nrm(ks[7], (L, G, P, H), (2 * H) ** -0.5),
        'ssm_c_re': nrm(ks[8], (L, G, H, P), P ** -0.5),
        'ssm_c_im': nrm(ks[9], (L, G, H, P), P ** -0.5),
        'ssm_d': nrm(ks[10], (L, SSM_WIDTH), 1.0),
        'ssm_w_glu': nrm(ks[11], (L, SSM_WIDTH, 2 * SSM_WIDTH), SSM_WIDTH ** -0.5),
        'cmp_pe_k': nrm(ks[12], (L, CMP_BLOCK, HEAD_DIM), 0.02),
        'cmp_w1_k': nrm(ks[13], (L, cmp_in, CMP_HIDDEN), cmp_in ** -0.5),
        'cmp_b1_k': nrm(ks[14], (L, CMP_HIDDEN), 0.01),
        'cmp_w2_k': nrm(ks[15], (L, CMP_HIDDEN, HEAD_DIM), CMP_HIDDEN ** -0.5),
        'cmp_pe_v': nrm(ks[16], (L, CMP_BLOCK, HEAD_DIM), 0.02),
        'cmp_w1_v': nrm(ks[17], (L, cmp_in, CMP_HIDDEN), cmp_in ** -0.5),
        'cmp_b1_v': nrm(ks[18], (L, CMP_HIDDEN), 0.01),
        'cmp_w2_v': nrm(ks[19], (L, CMP_HIDDEN, HEAD_DIM), CMP_HIDDEN ** -0.5),
        'w_branch_ssm': nrm(ks[20], (L, SSM_WIDTH, D_MODEL), SSM_WIDTH ** -0.5),
        'w_branch_nsa': nrm(ks[21], (L, Q_WIDTH, D_MODEL), Q_WIDTH ** -0.5),
        'w_out': nrm(ks[22], (L, D_MODEL, D_MODEL), D_MODEL ** -0.5),
        'norm_ffn': 1.0 + nrm(ks[23], (L, D_MODEL), 0.01),
        'w_ffn_in': nrm(ks[24], (L, D_MODEL, 2 * D_FF), D_MODEL ** -0.5),
        'ffn_conv_w': nrm(ks[25], (L, CONV_WIDTH, D_FF), CONV_WIDTH ** -0.5),
        'ffn_conv_b': nrm(ks[26], (L, D_FF), 0.01),
        'w_ffn_out': nrm(ks[27], (L, D_FF, D_MODEL), D_FF ** -0.5),
        'norm_final': 1.0 + nrm(ks[28], (D_MODEL,), 0.01),
    }


def reference(x, norm_mix, w_in, ssm_a_re, ssm_a_im, ssm_log_dt, ssm_b_re, ssm_b_im, ssm_c_re, ssm_c_im,
              ssm_d, ssm_w_glu, cmp_pe_k, cmp_w1_k, cmp_b1_k, cmp_w2_k, cmp_pe_v, cmp_w1_v, cmp_b1_v,
              cmp_w2_v, w_branch_ssm, w_branch_nsa, w_out, norm_ffn, w_ffn_in, ffn_conv_w, ffn_conv_b,
              w_ffn_out, norm_final):
    cos, sin = rope_tables(x.shape[1])
    for l in range(DEPTH):
        h = rmsnorm(x, norm_mix[l])
        x = x + hybrid_mixer(h, cos, sin, w_in[l], ssm_a_re[l], ssm_a_im[l], ssm_log_dt[l],
                             ssm_b_re[l], ssm_b_im[l], ssm_c_re[l], ssm_c_im[l], ssm_d[l], ssm_w_glu[l],
                             cmp_pe_k[l], cmp_w1_k[l], cmp_b1_k[l], cmp_w2_k[l],
                             cmp_pe_v[l], cmp_w1_v[l], cmp_b1_v[l], cmp_w2_v[l],
                             w_branch_ssm[l], w_branch_nsa[l], w_out[l])
        h = rmsnorm(x, norm_ffn[l])
        x = x + conv_ffn(h, w_ffn_in[l], ffn_conv_w[l], ffn_conv_b[l], w_ffn_out[l])
    return rmsnorm(x, norm_final)
```

```python
import functools
import math

import numpy as np
import jax
import jax.numpy as jnp
from jax import lax
from jax.experimental import pallas as pl
from jax.experimental.pallas import tpu as pltpu

F32 = jnp.float32
BF16 = jnp.bfloat16

D_MODEL = 1024
SSM_WIDTH = 512
SSM_GROUP = 16
SSM_GROUPS = 32
SSM_STATE = 64
N_Q_HEADS = 8
N_KV_HEADS = 2
HEAD_DIM = 64
GQA_GROUP = 4
CMP_BLOCK = 32
CMP_STRIDE = 16
CMP_HIDDEN = 256
SEL_BLOCK = 64
N_SEL = 16
WINDOW = 512
N_NSA_BRANCH = 3
D_FF = 2816
CONV_WIDTH = 3
RMS_EPS = 1e-6
NEG_INF = -1e30
FORCE_BONUS = 1e4
ROPE_THETA = 10000.0
Q_WIDTH = N_Q_HEADS * HEAD_DIM
KV_WIDTH = N_KV_HEADS * HEAD_DIM

LANES = 128
SUBLANES = 8
MXU_DIM = 256
VMEM_LIMIT = 56 * 1024 * 1024

ROW_TILE = 512
S5_TIME_TILE = 64
S5_CHAN_CHUNK = 1024
KEY_TILE = MXU_DIM
FF_CHUNK = MXU_DIM

_C_U = 0
_C_Q = _C_U + SSM_WIDTH
_C_KCMP = _C_Q + Q_WIDTH
_C_VCMP = _C_KCMP + KV_WIDTH
_C_KSEL = _C_VCMP + KV_WIDTH
_C_KWIN = _C_KSEL + KV_WIDTH
_C_VSEL = _C_KWIN + KV_WIDTH
_C_VWIN = _C_VSEL + KV_WIDTH
_C_GATE = _C_VWIN + KV_WIDTH
_C_GSSM = _C_GATE + LANES
_C_GNSA = _C_GSSM + D_MODEL
IN_WIDTH_PAD = _C_GNSA + D_MODEL


def _cparams(*sem):
    return pltpu.CompilerParams(dimension_semantics=sem, vmem_limit_bytes=VMEM_LIMIT)


def _gelu(x):
    return x * (0.5 * (1.0 + jnp.tanh(math.sqrt(2.0 / math.pi) * (x + 0.044715 * (x * x * x)))))


def _rms(x, g):
    ms = jnp.mean(x * x, axis=-1, keepdims=True)
    return (x * lax.rsqrt(ms + RMS_EPS)) * g


def _const_spec(shape):
    nd = len(shape)
    return pl.BlockSpec(shape, lambda *_: (0,) * nd)


def _inproj_kernel(x_ref, g_ref, w_ref, cos_ref, sin_ref,
                   u_ref, q_ref, kvc_ref, kvsw_ref, gate_ref, sgs_ref, sgn_ref):
    tm = x_ref.shape[0]
    h = _rms(x_ref[...], g_ref[...]).astype(BF16)

    def proj(c0, n):
        return jnp.dot(h, w_ref[:, c0:c0 + n], preferred_element_type=F32)

    cos = cos_ref[...]
    sin = sin_ref[...]
    lane = lax.broadcasted_iota(jnp.int32, (tm, LANES), 1)
    first_half = (lane & (HEAD_DIM // 2)) == 0

    def rope(v):
        swapped = jnp.where(first_half, pltpu.roll(v, LANES - HEAD_DIM // 2, 1),
                            pltpu.roll(v, HEAD_DIM // 2, 1))
        return v * cos + swapped * sin

    u_ref[...] = proj(_C_U, SSM_WIDTH)
    for c in range(Q_WIDTH // LANES):
        q_ref[:, c * LANES:(c + 1) * LANES] = rope(proj(_C_Q + c * LANES, LANES)).astype(BF16)
    kvc_ref[:, 0:LANES] = rope(proj(_C_KCMP, LANES))
    kvc_ref[:, LANES:2 * LANES] = proj(_C_VCMP, LANES)
    kvsw_ref[:, 0:LANES] = rope(proj(_C_KSEL, LANES)).astype(BF16)
    kvsw_ref[:, LANES:2 * LANES] = rope(proj(_C_KWIN, LANES)).astype(BF16)
    kvsw_ref[:, 2 * LANES:4 * LANES] = proj(_C_VSEL, 2 * LANES).astype(BF16)
    gate_ref[...] = jax.nn.sigmoid(proj(_C_GATE, LANES))
    for c in range(D_MODEL // MXU_DIM):
        sgs_ref[:, c * MXU_DIM:(c + 1) * MXU_DIM] = jax.nn.sigmoid(proj(_C_GSSM + c * MXU_DIM, MXU_DIM))
        sgn_ref[:, c * MXU_DIM:(c + 1) * MXU_DIM] = jax.nn.sigmoid(proj(_C_GNSA + c * MXU_DIM, MXU_DIM))


def _inproj(x2, g, w, cos, sin, seq):
    t = x2.shape[0]
    tm = ROW_TILE
    spt = seq // tm
    row = lambda i: (i, 0)
    outs = [
        jax.ShapeDtypeStruct((t, SSM_WIDTH), F32),
        jax.ShapeDtypeStruct((t, Q_WIDTH), BF16),
        jax.ShapeDtypeStruct((t, 2 * KV_WIDTH), F32),
        jax.ShapeDtypeStruct((t, 4 * KV_WIDTH), BF16),
        jax.ShapeDtypeStruct((t, LANES), F32),
        jax.ShapeDtypeStruct((t, D_MODEL), F32),
        jax.ShapeDtypeStruct((t, D_MODEL), F32),
    ]
    return pl.pallas_call(
        _inproj_kernel,
        grid=(t // tm,),
        in_specs=[
            pl.BlockSpec((tm, D_MODEL), row),
            _const_spec((1, D_MODEL)),
            _const_spec((D_MODEL, IN_WIDTH_PAD)),
            pl.BlockSpec((tm, LANES), lambda i: (i % spt, 0)),
            pl.BlockSpec((tm, LANES), lambda i: (i % spt, 0)),
        ],
        out_specs=[pl.BlockSpec((tm, o.shape[1]), row) for o in outs],
        out_shape=outs,
        compiler_params=_cparams("parallel"),
        name="inproj",
    )(x2, g, w, cos, sin)


def _prep_w_in(w_in):
    o = np.cumsum((0, SSM_WIDTH, Q_WIDTH) + (KV_WIDTH,) * 6 + (N_Q_HEADS * N_NSA_BRANCH, D_MODEL, D_MODEL))
    u, q, k_cmp, v_cmp, k_sel, v_sel, k_win, v_win, gate, g_ssm, g_nsa = (
        w_in[:, o[i]:o[i + 1]] for i in range(11))
    gate = gate.reshape(D_MODEL, N_KV_HEADS, GQA_GROUP, N_NSA_BRANCH).transpose(0, 1, 3, 2)
    gate = gate.reshape(D_MODEL, N_KV_HEADS, N_NSA_BRANCH * GQA_GROUP)
    gate = jnp.pad(gate, ((0, 0), (0, 0), (0, HEAD_DIM - N_NSA_BRANCH * GQA_GROUP))).reshape(D_MODEL, LANES)
    w = jnp.concatenate([u, q, k_cmp, v_cmp, k_sel, k_win, v_sel, v_win, gate, g_ssm, g_nsa], axis=1)
    return w.astype(BF16)


def _rope_tables(seq):
    inv_freq = ROPE_THETA ** (-jnp.arange(0, HEAD_DIM, 2, dtype=F32) / HEAD_DIM)
    ang = jnp.arange(seq, dtype=F32)[:, None] * inv_freq[None, :]
    cos, sin = jnp.cos(ang), jnp.sin(ang)
    return jnp.tile(cos, (1, 4)), jnp.tile(jnp.concatenate([-sin, sin], axis=1), (1, 2))


def _s5_kernel(u_ref, bre_ref, bim_ref, cre_ref, cim_ref, ar_ref, ai_ref, d_ref, wglu_ref,
               z_ref, xr_s, xi_s, str_s, sti_s):
    ts, nb, _ = u_ref.shape
    rows = ts * nb
    half_in = SSM_WIDTH // 2
    half_st = SSM_GROUPS * SSM_STATE // 2

    @pl.when(pl.program_id(0) == 0)
    def _():
        str_s[...] = jnp.zeros_like(str_s)
        sti_s[...] = jnp.zeros_like(sti_s)

    u2 = u_ref[...].reshape(rows, SSM_WIDTH)
    ub = u2.astype(BF16)
    for hf in range(2):
        uh = ub[:, hf * half_in:(hf + 1) * half_in]
        xr_s[:, hf * half_st:(hf + 1) * half_st] = jnp.dot(uh, bre_ref[hf], preferred_element_type=F32)
        xi_s[:, hf * half_st:(hf + 1) * half_st] = jnp.dot(uh, bim_ref[hf], preferred_element_type=F32)

    for c in range(2 * half_st // S5_CHAN_CHUNK):
        sl = slice(c * S5_CHAN_CHUNK, (c + 1) * S5_CHAN_CHUNK)
        ar = jnp.broadcast_to(ar_ref[:, sl], (nb, S5_CHAN_CHUNK))
        ai = jnp.broadcast_to(ai_ref[:, sl], (nb, S5_CHAN_CHUNK))

        def step(t, carry, sl=sl, ar=ar, ai=ai):
            xr, xi = carry
            r0 = pl.multiple_of(t * nb, nb)
            nr = ar * xr - ai * xi + xr_s[pl.ds(r0, nb), sl]
            ni = ar * xi + ai * xr + xi_s[pl.ds(r0, nb), sl]
            xr_s[pl.ds(r0, nb), sl] = nr
            xi_s[pl.ds(r0, nb), sl] = ni
            return nr, ni

        xr, xi = lax.fori_loop(0, ts, step, (str_s[:, sl], sti_s[:, sl]), unroll=8)
        str_s[:, sl] = xr
        sti_s[:, sl] = xi

    ys = []
    for hf in range(2):
        st = slice(hf * half_st, (hf + 1) * half_st)
        ys.append(jnp.dot(xr_s[:, st].astype(BF16), cre_ref[hf], preferred_element_type=F32)
                  - jnp.dot(xi_s[:, st].astype(BF16), cim_ref[hf], preferred_element_type=F32))
    y = jnp.concatenate(ys, axis=1) + d_ref[...] * u2
    zz = jnp.dot(_gelu(y).astype(BF16), wglu_ref[...], preferred_element_type=F32)
    z = zz[:, :SSM_WIDTH] * jax.nn.sigmoid(zz[:, SSM_WIDTH:])
    z_ref[...] = z.reshape(ts, nb, SSM_WIDTH)


def _s5(u_t, p):
    seq, nb, _ = u_t.shape
    assert nb == SUBLANES, "the scan maps the batch onto the 8 sublanes"
    ts = S5_TIME_TILE
    nst = SSM_GROUPS * SSM_STATE
    return pl.pallas_call(
        _s5_kernel,
        grid=(seq // ts,),
        in_specs=[
            pl.BlockSpec((ts, nb, SSM_WIDTH), lambda i: (i, 0, 0)),
            _const_spec(p["bre"].shape), _const_spec(p["bim"].shape),
            _const_spec(p["cre"].shape), _const_spec(p["cim"].shape),
            _const_spec((1, nst)), _const_spec((1, nst)),
            _const_spec((1, SSM_WIDTH)), _const_spec((SSM_WIDTH, 2 * SSM_WIDTH)),
        ],
        out_specs=pl.BlockSpec((ts, nb, SSM_WIDTH), lambda i: (i, 0, 0)),
        out_shape=jax.ShapeDtypeStruct((seq, nb, SSM_WIDTH), F32),
        scratch_shapes=[
            pltpu.VMEM((ts * nb, nst), F32), pltpu.VMEM((ts * nb, nst), F32),
            pltpu.VMEM((nb, nst), F32), pltpu.VMEM((nb, nst), F32),
        ],
        compiler_params=_cparams("arbitrary"),
        name="s5",
    )(u_t, p["bre"], p["bim"], p["cre"], p["cim"], p["ar"], p["ai"], p["d"], p["wglu"])


def _prep_s5(a_re, a_im, log_dt, b_re, b_im, c_re, c_im, d, w_glu):
    dt = jnp.exp(log_dt)[:, None]
    mag = jnp.exp(a_re * dt)
    ab_re = mag * jnp.cos(a_im * dt)
    ab_im = mag * jnp.sin(a_im * dt)
    den = a_re * a_re + a_im * a_im
    f_re = ((ab_re - 1.0) * a_re + ab_im * a_im) / den
    f_im = (ab_im * a_re - (ab_re - 1.0) * a_im) / den
    fb_re = f_re[:, :, None] * b_re - f_im[:, :, None] * b_im
    fb_im = f_re[:, :, None] * b_im + f_im[:, :, None] * b_re
    gh = SSM_GROUPS // 2
    eye = jnp.eye(gh, dtype=F32)

    def blk_in(w):
        w = w.reshape(2, gh, SSM_STATE, SSM_GROUP)
        return jnp.einsum("kgph,gj->kghjp", w, eye).reshape(2, gh * SSM_GROUP, gh * SSM_STATE).astype(BF16)

    def blk_out(w):
        w = w.reshape(2, gh, SSM_GROUP, SSM_STATE)
        return jnp.einsum("kghp,gj->kgpjh", w, eye).reshape(2, gh * SSM_STATE, gh * SSM_GROUP).astype(BF16)

    return dict(bre=blk_in(fb_re), bim=blk_in(fb_im), cre=blk_out(c_re), cim=blk_out(c_im),
                ar=ab_re.reshape(1, -1), ai=ab_im.reshape(1, -1), d=d.reshape(1, -1),
                wglu=w_glu.astype(BF16))


def _compress_kernel(r_ref, pe_ref, w1_ref, b1_ref, w2_ref, o_ref):
    r = r_ref[0, 0]
    n = r.shape[0]
    half = CMP_STRIDE * HEAD_DIM
    h1 = jnp.dot((r + pe_ref[0, 0:1, :]).astype(BF16), w1_ref[0, :half, :], preferred_element_type=F32)
    h2 = jnp.dot((r + pe_ref[0, 1:2, :]).astype(BF16), w1_ref[0, half:, :], preferred_element_type=F32)
    hid = h1 + pltpu.roll(h2, n - 1, 0) + b1_ref[0]
    out = jnp.dot(_gelu(hid).astype(BF16), w2_ref[0], preferred_element_type=F32)
    rowi = lax.broadcasted_iota(jnp.int32, out.shape, 0)
    o_ref[0, 0] = jnp.where(rowi < n - 1, out, 0.0)


def _compress(r, pe, w1, b1, w2):
    bsz, four, n, width = r.shape
    return pl.pallas_call(
        _compress_kernel,
        grid=(bsz, four),
        in_specs=[
            pl.BlockSpec((1, 1, n, width), lambda b, c: (b, c, 0, 0)),
            pl.BlockSpec((1, 2, width), lambda b, c: (c // N_KV_HEADS, 0, 0)),
            pl.BlockSpec((1, 2 * width, CMP_HIDDEN), lambda b, c: (c // N_KV_HEADS, 0, 0)),
            pl.BlockSpec((1, 1, CMP_HIDDEN), lambda b, c: (c // N_KV_HEADS, 0, 0)),
            pl.BlockSpec((1, CMP_HIDDEN, HEAD_DIM), lambda b, c: (c // N_KV_HEADS, 0, 0)),
        ],
        out_specs=pl.BlockSpec((1, 1, n, HEAD_DIM), lambda b, c: (b, c, 0, 0)),
        out_shape=jax.ShapeDtypeStruct((bsz, four, n, HEAD_DIM), F32),
        compiler_params=_cparams("parallel", "parallel"),
        name="compress",
    )(r, pe, w1, b1, w2)


def _nsa_kernel(q_ref, kc_ref, vc_ref, selm_ref, ks_ref, vs_ref, kw_ref, vw_ref, gate_ref,
                o_ref, lhs_s, m_s, acc_s):
    tq = q_ref.shape[2]
    rows = GQA_GROUP * tq
    n_cmp = kc_ref.shape[3]
    n_blk = selm_ref.shape[1]
    qi = pl.program_id(2)
    s0 = qi * tq

    q4 = q_ref[0].reshape(rows, LANES) * (HEAD_DIM ** -0.5)
    t_col = s0 + (lax.broadcasted_iota(jnp.int32, (rows, 1), 0) & (tq - 1))

    s = jnp.dot(q4, kc_ref[0, 0], preferred_element_type=F32)
    n_idx = lax.broadcasted_iota(jnp.int32, (rows, n_cmp), 1)
    cmask = (n_idx * CMP_STRIDE + (CMP_BLOCK - 1)) <= t_col
    s = jnp.where(cmask, s, NEG_INF)
    e = jnp.exp(s - jnp.max(s, axis=1, keepdims=True))
    p = jnp.where(cmask, e, 0.0) / jnp.sum(e, axis=1, keepdims=True)
    o_cmp = jnp.dot(p.astype(BF16), vc_ref[0, 0], preferred_element_type=F32)

    pg = p[0:tq] + p[tq:2 * tq] + p[2 * tq:3 * tq] + p[3 * tq:4 * tq]
    selm = selm_ref[...]
    hi = pg.astype(BF16)
    r1 = pg - hi.astype(F32)
    mid = r1.astype(BF16)
    lo = (r1 - mid.astype(F32)).astype(BF16)
    bs = (jnp.dot(hi, selm, preferred_element_type=F32) + jnp.dot(mid, selm, preferred_element_type=F32)
          + jnp.dot(lo, selm, preferred_element_type=F32))
    tq_col = s0 + lax.broadcasted_iota(jnp.int32, (tq, 1), 0)
    j_idx = lax.broadcasted_iota(jnp.int32, (tq, n_blk), 1)
    cur = tq_col >> int(math.log2(SEL_BLOCK))
    forced = (j_idx == 0) | (j_idx == cur) | (j_idx == cur - 1)
    causal_b = j_idx * SEL_BLOCK <= tq_col
    bs = jnp.where(causal_b, bs + jnp.where(forced, FORCE_BONUS, 0.0), NEG_INF)

    work = bs.T
    idx = lax.broadcasted_iota(jnp.int32, (n_blk, tq), 0).astype(F32)
    picked = jnp.zeros((n_blk, tq), F32)
    for _ in range(min(N_SEL, n_blk)):
        mx = jnp.max(work, axis=0, keepdims=True)
        first = jnp.min(jnp.where(work == mx, idx, float(n_blk)), axis=0, keepdims=True)
        hit = idx == first
        picked = jnp.where(hit, 1.0, picked)
        work = jnp.where(hit, -jnp.inf, work)
    sel = (picked.T > 0.0) & causal_b
    bias = jnp.where(sel, 0.0, NEG_INF).astype(BF16)
    lhs_s[:, 0:LANES] = q4
    for g in range(GQA_GROUP):
        lhs_s[g * tq:(g + 1) * tq, LANES:2 * LANES] = bias

    m_s[...] = jnp.full(m_s.shape, NEG_INF, F32)
    acc_s[...] = jnp.zeros(acc_s.shape, F32)

    def sel_tile(kt, diag):
        st = jnp.dot(lhs_s[...], ks_ref[0, 0, kt], preferred_element_type=F32)
        if diag:
            pos = kt * KEY_TILE + lax.broadcasted_iota(jnp.int32, st.shape, 1)
            st = jnp.where(pos <= t_col, st, NEG_INF)
        m_old = m_s[...]
        m_new = jnp.maximum(m_old, jnp.max(st, axis=1, keepdims=True))
        pt = jnp.exp(st - m_new)
        acc_s[...] = jnp.exp(m_old - m_new) * acc_s[...] + jnp.dot(
            pt.astype(BF16), vs_ref[0, 0, kt], preferred_element_type=F32)
        m_s[...] = m_new

    def sel_body(kt, carry):
        sel_tile(kt, False)
        return carry

    lax.fori_loop(0, qi, sel_body, 0)
    sel_tile(qi, True)
    acc = acc_s[...]
    o_sel = acc[:, :HEAD_DIM] / acc[:, HEAD_DIM:HEAD_DIM + 1]

    n_wt = WINDOW // KEY_TILE + 1
    wt0 = jnp.maximum(qi - WINDOW // KEY_TILE, 0)
    sw = jnp.concatenate(
        [jnp.dot(q4, kw_ref[0, 0, wt0 + i], preferred_element_type=F32) for i in range(n_wt)], axis=1)
    pos = wt0 * KEY_TILE + lax.broadcasted_iota(jnp.int32, sw.shape, 1)
    wmask = (pos <= t_col) & (pos > t_col - WINDOW)
    sw = jnp.where(wmask, sw, NEG_INF)
    pw = jnp.exp(sw - jnp.max(sw, axis=1, keepdims=True)).astype(BF16)
    accw = jnp.dot(pw[:, 0:KEY_TILE], vw_ref[0, 0, wt0], preferred_element_type=F32)
    for i in range(1, n_wt):
        accw += jnp.dot(pw[:, i * KEY_TILE:(i + 1) * KEY_TILE], vw_ref[0, 0, wt0 + i],
                        preferred_element_type=F32)
    o_win = accw[:, :HEAD_DIM] / accw[:, HEAD_DIM:HEAD_DIM + 1]

    gate = gate_ref[0, 0]
    outs = []
    for g in range(GQA_GROUP):
        rs = slice(g * tq, (g + 1) * tq)
        outs.append(gate[:, g:g + 1] * o_cmp[rs, :HEAD_DIM]
                    + gate[:, GQA_GROUP + g:GQA_GROUP + g + 1] * o_sel[rs]
                    + gate[:, 2 * GQA_GROUP + g:2 * GQA_GROUP + g + 1] * o_win[rs])
    o_ref[0] = jnp.concatenate(outs, axis=1)


def _nsa(q_r, kc_t, vc_a, selm, ks_a, vs_a, kw_t, vw_a, gate_r):
    bsz, _, seq, _ = q_r.shape
    tq = KEY_TILE
    nt = seq // tq
    n_cmp = kc_t.shape[3]
    rows = GQA_GROUP * tq
    bh = lambda b, h, i: (b, h, 0, 0)
    bh5 = lambda b, h, i: (b, h, 0, 0, 0)
    return pl.pallas_call(
        _nsa_kernel,
        grid=(bsz, N_KV_HEADS, nt),
        in_specs=[
            pl.BlockSpec((1, GQA_GROUP, tq, LANES), lambda b, h, i: (b, h, i, 0)),
            pl.BlockSpec((1, 1, LANES, n_cmp), bh),
            pl.BlockSpec((1, 1, n_cmp, LANES), bh),
            _const_spec(selm.shape),
            pl.BlockSpec((1, 1, nt, 2 * LANES, tq), bh5),
            pl.BlockSpec((1, 1, nt, tq, LANES), bh5),
            pl.BlockSpec((1, 1, nt, LANES, tq), bh5),
            pl.BlockSpec((1, 1, nt, tq, LANES), bh5),
            pl.BlockSpec((1, 1, tq, HEAD_DIM), lambda b, h, i: (b, h, i, 0)),
        ],
        out_specs=pl.BlockSpec((1, tq, GQA_GROUP * HEAD_DIM), lambda b, h, i: (b, i, h)),
        out_shape=jax.ShapeDtypeStruct((bsz, seq, Q_WIDTH), F32),
        scratch_shapes=[
            pltpu.VMEM((rows, 2 * LANES), BF16),
            pltpu.VMEM((rows, 1), F32),
            pltpu.VMEM((rows, LANES), F32),
        ],
        compiler_params=_cparams("parallel", "parallel", "arbitrary"),
        name="nsa",
    )(q_r, kc_t, vc_a, selm, ks_a, vs_a, kw_t, vw_a, gate_r)


def _sel_fold_matrix(n_cmp_pad, n_blk):
    ratio = SEL_BLOCK // CMP_STRIDE
    span = ratio + CMP_BLOCK // CMP_STRIDE - 1
    pad_l = CMP_BLOCK // CMP_STRIDE - 1
    i = np.arange(n_cmp_pad)[:, None]
    j = np.arange(n_blk)[None, :]
    lo = ratio * j - pad_l
    return jnp.asarray(((i >= lo) & (i < lo + span)), dtype=BF16)


def _block_onehot(nt, n_blk):
    key_blk = (np.arange(nt)[:, None] * KEY_TILE + np.arange(KEY_TILE)[None, :]) // SEL_BLOCK
    return jnp.asarray(key_blk[:, None, :] == np.arange(n_blk)[None, :, None], dtype=BF16)


def _heads_first(a, bsz, seq):
    return a.reshape(bsz, seq, -1, HEAD_DIM).transpose(0, 2, 1, 3)


def _key_tiles_t(k, nt, depth):
    b, h, _, hd = k.shape
    kt = k.reshape(b, h, nt, KEY_TILE, hd).transpose(0, 1, 2, 4, 3)
    return jnp.pad(kt, ((0, 0),) * 3 + ((0, depth - hd), (0, 0)))


def _value_tiles(v, nt):
    b, h, s, hd = v.shape
    va = jnp.concatenate([v, jnp.ones((b, h, s, 1), v.dtype), jnp.zeros((b, h, s, LANES - hd - 1), v.dtype)], -1)
    return va.reshape(b, h, nt, KEY_TILE, LANES)


def _merge_kernel(x_ref, z_ref, a_ref, sgs_ref, sgn_ref, wbs_ref, wbn_ref, wo_ref, o_ref):
    y_ssm = jnp.dot(z_ref[...].astype(BF16), wbs_ref[...], preferred_element_type=F32)
    y_nsa = jnp.dot(a_ref[...].astype(BF16), wbn_ref[...], preferred_element_type=F32)
    merged = sgs_ref[...] * y_ssm + sgn_ref[...] * y_nsa
    o_ref[...] = x_ref[...] + jnp.dot(merged.astype(BF16), wo_ref[...], preferred_element_type=F32)


def _merge(x2, z, a, sgs, sgn, wbs, wbn, wo):
    t = x2.shape[0]
    tm = ROW_TILE
    row = lambda i: (i, 0)
    return pl.pallas_call(
        _merge_kernel,
        grid=(t // tm,),
        in_specs=[
            pl.BlockSpec((tm, D_MODEL), row), pl.BlockSpec((tm, SSM_WIDTH), row),
            pl.BlockSpec((tm, Q_WIDTH), row), pl.BlockSpec((tm, D_MODEL), row),
            pl.BlockSpec((tm, D_MODEL), row),
            _const_spec(wbs.shape), _const_spec(wbn.shape), _const_spec(wo.shape),
        ],
        out_specs=pl.BlockSpec((tm, D_MODEL), row),
        out_shape=jax.ShapeDtypeStruct((t, D_MODEL), F32),
        compiler_params=_cparams("parallel"),
        name="merge",
    )(x2, z, a, sgs, sgn, wbs, wbn, wo)


def _ffn_kernel(x_ref, xh_ref, g_ref, wa_ref, wb_ref, cw_ref, cb_ref, wo_ref, gf_ref, o_ref, a_s,
                *, tiles_per_seq, final):
    tm = x_ref.shape[0]
    halo = xh_ref.shape[0]
    first = (pl.program_id(0) % tiles_per_seq) == 0
    x = x_ref[...]
    g = g_ref[...]
    h = _rms(x, g).astype(BF16)
    hh = _rms(xh_ref[...], g).astype(BF16)
    keep_halo = jnp.where(first, 0.0, 1.0)
    acc = jnp.zeros((tm, D_MODEL), F32)
    for c in range(D_FF // FF_CHUNK):
        cs = slice(c * FF_CHUNK, (c + 1) * FF_CHUNK)
        a_s[0:halo, :] = jnp.dot(hh, wa_ref[:, cs], preferred_element_type=F32) * keep_halo
        a_s[halo:halo + tm, :] = jnp.dot(h, wa_ref[:, cs], preferred_element_type=F32)
        b = jnp.dot(h, wb_ref[:, cs], preferred_element_type=F32)
        cw = cw_ref[:, cs]
        conv = cb_ref[:, cs]
        for k in range(CONV_WIDTH):
            off = halo - (CONV_WIDTH - 1) + k
            conv = conv + cw[k:k + 1, :] * a_s[off:off + tm, :]
        act = (_gelu(conv) * b).astype(BF16)
        acc = acc + jnp.dot(act, wo_ref[cs, :], preferred_element_type=F32)
    y = x + acc
    if final:
        y = _rms(y, gf_ref[...])
    o_ref[...] = y


def _ffn(x2, g, wa, wb, cw, cb, wo, gf, seq, final):
    t = x2.shape[0]
    tm = ROW_TILE
    halo = SUBLANES
    tps = seq // tm
    row = lambda i: (i, 0)
    return pl.pallas_call(
        functools.partial(_ffn_kernel, tiles_per_seq=tps, final=final),
        grid=(t // tm,),
        in_specs=[
            pl.BlockSpec((tm, D_MODEL), row),
            pl.BlockSpec((halo, D_MODEL), lambda i: (jnp.maximum(i * (tm // halo) - 1, 0), 0)),
            _const_spec((1, D_MODEL)),
            _const_spec(wa.shape), _const_spec(wb.shape),
            _const_spec(cw.shape), _const_spec(cb.shape), _const_spec(wo.shape),
            _const_spec((1, D_MODEL)),
        ],
        out_specs=pl.BlockSpec((tm, D_MODEL), row),
        out_shape=jax.ShapeDtypeStruct((t, D_MODEL), F32),
        scratch_shapes=[pltpu.VMEM((tm + halo, FF_CHUNK), F32)],
        compiler_params=_cparams("parallel"),
        name="ffn",
    )(x2, x2, g, wa, wb, cw, cb, wo, gf)


def _mixer_layer(x2, bsz, seq, cos, sin, norm_g, w_in, s5p, pe, w1, b1, w2, wbs, wbn, wo, selm, onehot):
    nt = seq // KEY_TILE
    u, q, kvc, kvsw, gate, sgs, sgn = _inproj(x2, norm_g, w_in, cos, sin, seq)

    u_t = u.reshape(bsz, seq, SSM_WIDTH).transpose(1, 0, 2)
    z = _s5(u_t, s5p).transpose(1, 0, 2).reshape(bsz * seq, SSM_WIDTH)

    n_rows = seq // CMP_STRIDE
    r = _heads_first(kvc, bsz, seq).reshape(bsz, 2 * N_KV_HEADS, n_rows, CMP_STRIDE * HEAD_DIM)
    kvc_c = _compress(r, pe, w1, b1, w2)
    kc_t = jnp.pad(kvc_c[:, :N_KV_HEADS].transpose(0, 1, 3, 2).astype(BF16),
                   ((0, 0), (0, 0), (0, LANES - HEAD_DIM), (0, 0)))
    vc_a = jnp.pad(kvc_c[:, N_KV_HEADS:].astype(BF16), ((0, 0), (0, 0), (0, 0), (0, LANES - HEAD_DIM)))
    kv = _heads_first(kvsw, bsz, seq)
    ks_t = _key_tiles_t(kv[:, 0:2], nt, LANES)
    ks_a = jnp.concatenate([ks_t, jnp.broadcast_to(onehot, (bsz, N_KV_HEADS) + onehot.shape)], axis=3)
    kw_t = _key_tiles_t(kv[:, 2:4], nt, LANES)
    vs_a = _value_tiles(kv[:, 4:6], nt)
    vw_a = _value_tiles(kv[:, 6:8], nt)
    q_r = jnp.pad(_heads_first(q, bsz, seq), ((0, 0), (0, 0), (0, 0), (0, LANES - HEAD_DIM)))
    gate_r = _heads_first(gate, bsz, seq)
    a = _nsa(q_r, kc_t, vc_a, selm, ks_a, vs_a, kw_t, vw_a, gate_r).reshape(bsz * seq, Q_WIDTH)

    return _merge(x2, z, a, sgs, sgn, wbs, wbn, wo)


def kernel(x, norm_mix, w_in, ssm_a_re, ssm_a_im, ssm_log_dt, ssm_b_re, ssm_b_im, ssm_c_re, ssm_c_im,
           ssm_d, ssm_w_glu, cmp_pe_k, cmp_w1_k, cmp_b1_k, cmp_w2_k, cmp_pe_v, cmp_w1_v, cmp_b1_v,
           cmp_w2_v, w_branch_ssm, w_branch_nsa, w_out, norm_ffn, w_ffn_in, ffn_conv_w, ffn_conv_b,
           w_ffn_out, norm_final):
    bsz, seq, _ = x.shape
    depth = norm_mix.shape[0]
    assert seq % ROW_TILE == 0 and seq >= (WINDOW // KEY_TILE + 1) * KEY_TILE
    cos, sin = _rope_tables(seq)
    assert N_SEL <= seq // SEL_BLOCK <= LANES, "selection blocks are mapped onto one vreg of lanes"
    selm = _sel_fold_matrix(seq // CMP_STRIDE, LANES)
    onehot = _block_onehot(seq // KEY_TILE, LANES)
    half = CMP_STRIDE * HEAD_DIM
    x2 = x.reshape(bsz * seq, D_MODEL)
    for l in range(depth):
        s5p = _prep_s5(ssm_a_re[l], ssm_a_im[l], ssm_log_dt[l], ssm_b_re[l], ssm_b_im[l],
                       ssm_c_re[l], ssm_c_im[l], ssm_d[l], ssm_w_glu[l])
        pe = jnp.stack([cmp_pe_k[l], cmp_pe_v[l]]).reshape(2, 2, half)
        w1 = jnp.stack([cmp_w1_k[l], cmp_w1_v[l]]).astype(BF16)
        b1 = jnp.stack([cmp_b1_k[l], cmp_b1_v[l]]).reshape(2, 1, CMP_HIDDEN)
        w2 = jnp.stack([cmp_w2_k[l], cmp_w2_v[l]]).astype(BF16)
        x2 = _mixer_layer(x2, bsz, seq, cos, sin, norm_mix[l].reshape(1, -1), _prep_w_in(w_in[l]), s5p,
                          pe, w1, b1, w2, w_branch_ssm[l].astype(BF16), w_branch_nsa[l].astype(BF16),
                          w_out[l].astype(BF16), selm, onehot)
        wf = w_ffn_in[l].astype(BF16)
        x2 = _ffn(x2, norm_ffn[l].reshape(1, -1), wf[:, :D_FF], wf[:, D_FF:], ffn_conv_w[l],
                  ffn_conv_b[l].reshape(1, -1), w_ffn_out[l].astype(BF16), norm_final.reshape(1, -1),
                  seq, final=(l == depth - 1))
    return x2.reshape(bsz, seq, D_MODEL)
```

```python
import functools
import math

import numpy as np
import jax
import jax.numpy as jnp
from jax import lax
from jax.experimental import pallas as pl
from jax.experimental.pallas import tpu as pltpu

F32 = jnp.float32
BF16 = jnp.bfloat16

D_MODEL = 1024
SSM_WIDTH = 512
SSM_GROUP = 16
SSM_GROUPS = 32
SSM_STATE = 64
N_Q_HEADS = 8
N_KV_HEADS = 2
HEAD_DIM = 64
GQA_GROUP = 4
CMP_BLOCK = 32
CMP_STRIDE = 16
CMP_HIDDEN = 256
SEL_BLOCK = 64
N_SEL = 16
WINDOW = 512
N_NSA_BRANCH = 3
D_FF = 2816
CONV_WIDTH = 3
RMS_EPS = 1e-6
NEG_INF = -1e30
FORCE_BONUS = 1e4
ROPE_THETA = 10000.0
Q_WIDTH = N_Q_HEADS * HEAD_DIM
KV_WIDTH = N_KV_HEADS * HEAD_DIM

LANES = 128
SUBLANES = 8
MXU_DIM = 256
VMEM_LIMIT = 56 * 1024 * 1024

ROW_TILE = 512
S5_TIME_TILE = 64
S5_CHAN_CHUNK = 1024
KEY_TILE = MXU_DIM
FF_CHUNK = MXU_DIM
NSA_CHUNK = 128

_C_U = 0
_C_Q = _C_U + SSM_WIDTH
_C_KCMP = _C_Q + Q_WIDTH
_C_VCMP = _C_KCMP + KV_WIDTH
_C_KSEL = _C_VCMP + KV_WIDTH
_C_KWIN = _C_KSEL + KV_WIDTH
_C_VSEL = _C_KWIN + KV_WIDTH
_C_VWIN = _C_VSEL + KV_WIDTH
_C_GATE = _C_VWIN + KV_WIDTH
_C_GSSM = _C_GATE + LANES
_C_GNSA = _C_GSSM + D_MODEL
IN_WIDTH_PAD = _C_GNSA + D_MODEL


def _cparams(*sem):
    return pltpu.CompilerParams(dimension_semantics=sem, vmem_limit_bytes=VMEM_LIMIT)


def _gelu(x):
    return x * (0.5 * (1.0 + jnp.tanh(math.sqrt(2.0 / math.pi) * (x + 0.044715 * (x * x * x)))))


def _rms(x, g):
    ms = jnp.mean(x * x, axis=-1, keepdims=True)
    return (x * lax.rsqrt(ms + RMS_EPS)) * g


def _const_spec(shape):
    nd = len(shape)
    return pl.BlockSpec(shape, lambda *_: (0,) * nd)


def _inproj_kernel(x_ref, g_ref, w_ref, cos_ref, sin_ref,
                   u_ref, q_ref, kvc_ref, kvsw_ref, gate_ref, sgs_ref, sgn_ref):
    tm = x_ref.shape[0]
    h = _rms(x_ref[...], g_ref[...]).astype(BF16)

    def proj(c0, n):
        return jnp.dot(h, w_ref[:, c0:c0 + n], preferred_element_type=F32)

    cos = cos_ref[...]
    sin = sin_ref[...]
    lane = lax.broadcasted_iota(jnp.int32, (tm, LANES), 1)
    first_half = (lane & (HEAD_DIM // 2)) == 0

    def rope(v):
        swapped = jnp.where(first_half, pltpu.roll(v, LANES - HEAD_DIM // 2, 1),
                            pltpu.roll(v, HEAD_DIM // 2, 1))
        return v * cos + swapped * sin

    u_ref[...] = proj(_C_U, SSM_WIDTH)
    for c in range(Q_WIDTH // LANES):
        q_ref[:, c * LANES:(c + 1) * LANES] = rope(proj(_C_Q + c * LANES, LANES)).astype(BF16)
    kvc_ref[:, 0:LANES] = rope(proj(_C_KCMP, LANES))
    kvc_ref[:, LANES:2 * LANES] = proj(_C_VCMP, LANES)
    kvsw_ref[:, 0:LANES] = rope(proj(_C_KSEL, LANES)).astype(BF16)
    kvsw_ref[:, LANES:2 * LANES] = rope(proj(_C_KWIN, LANES)).astype(BF16)
    kvsw_ref[:, 2 * LANES:4 * LANES] = proj(_C_VSEL, 2 * LANES).astype(BF16)
    gate_ref[...] = jax.nn.sigmoid(proj(_C_GATE, LANES))
    for c in range(D_MODEL // MXU_DIM):
        sgs_ref[:, c * MXU_DIM:(c + 1) * MXU_DIM] = jax.nn.sigmoid(proj(_C_GSSM + c * MXU_DIM, MXU_DIM))
        sgn_ref[:, c * MXU_DIM:(c + 1) * MXU_DIM] = jax.nn.sigmoid(proj(_C_GNSA + c * MXU_DIM, MXU_DIM))


def _inproj(x2, g, w, cos, sin, seq):
    t = x2.shape[0]
    tm = ROW_TILE
    spt = seq // tm
    row = lambda i: (i, 0)
    outs = [
        jax.ShapeDtypeStruct((t, SSM_WIDTH), F32),
        jax.ShapeDtypeStruct((t, Q_WIDTH), BF16),
        jax.ShapeDtypeStruct((t, 2 * KV_WIDTH), F32),
        jax.ShapeDtypeStruct((t, 4 * KV_WIDTH), BF16),
        jax.ShapeDtypeStruct((t, LANES), F32),
        jax.ShapeDtypeStruct((t, D_MODEL), F32),
        jax.ShapeDtypeStruct((t, D_MODEL), F32),
    ]
    return pl.pallas_call(
        _inproj_kernel,
        grid=(t // tm,),
        in_specs=[
            pl.BlockSpec((tm, D_MODEL), row),
            _const_spec((1, D_MODEL)),
            _const_spec((D_MODEL, IN_WIDTH_PAD)),
            pl.BlockSpec((tm, LANES), lambda i: (i % spt, 0)),
            pl.BlockSpec((tm, LANES), lambda i: (i % spt, 0)),
        ],
        out_specs=[pl.BlockSpec((tm, o.shape[1]), row) for o in outs],
        out_shape=outs,
        compiler_params=_cparams("parallel"),
        name="inproj",
    )(x2, g, w, cos, sin)


def _prep_w_in(w_in):
    o = np.cumsum((0, SSM_WIDTH, Q_WIDTH) + (KV_WIDTH,) * 6 + (N_Q_HEADS * N_NSA_BRANCH, D_MODEL, D_MODEL))
    u, q, k_cmp, v_cmp, k_sel, v_sel, k_win, v_win, gate, g_ssm, g_nsa = (
        w_in[:, o[i]:o[i + 1]] for i in range(11))
    gate = gate.reshape(D_MODEL, N_KV_HEADS, GQA_GROUP, N_NSA_BRANCH).transpose(0, 1, 3, 2)
    gate = gate.reshape(D_MODEL, N_KV_HEADS, N_NSA_BRANCH * GQA_GROUP)
    gate = jnp.pad(gate, ((0, 0), (0, 0), (0, HEAD_DIM - N_NSA_BRANCH * GQA_GROUP))).reshape(D_MODEL, LANES)
    w = jnp.concatenate([u, q, k_cmp, v_cmp, k_sel, k_win, v_sel, v_win, gate, g_ssm, g_nsa], axis=1)
    return w.astype(BF16)


def _rope_tables(seq):
    inv_freq = ROPE_THETA ** (-jnp.arange(0, HEAD_DIM, 2, dtype=F32) / HEAD_DIM)
    ang = jnp.arange(seq, dtype=F32)[:, None] * inv_freq[None, :]
    cos, sin = jnp.cos(ang), jnp.sin(ang)
    return jnp.tile(cos, (1, 4)), jnp.tile(jnp.concatenate([-sin, sin], axis=1), (1, 2))


def _s5_kernel(u_ref, bre_ref, bim_ref, cre_ref, cim_ref, ar_ref, ai_ref, d_ref, wglu_ref,
               z_ref, xr_s, xi_s, str_s, sti_s):
    ts, nb, _ = u_ref.shape
    rows = ts * nb
    half_in = SSM_WIDTH // 2
    half_st = SSM_GROUPS * SSM_STATE // 2

    @pl.when(pl.program_id(0) == 0)
    def _():
        str_s[...] = jnp.zeros_like(str_s)
        sti_s[...] = jnp.zeros_like(sti_s)

    u2 = u_ref[...].reshape(rows, SSM_WIDTH)
    ub = u2.astype(BF16)
    for hf in range(2):
        uh = ub[:, hf * half_in:(hf + 1) * half_in]
        xr_s[:, hf * half_st:(hf + 1) * half_st] = jnp.dot(uh, bre_ref[hf], preferred_element_type=F32)
        xi_s[:, hf * half_st:(hf + 1) * half_st] = jnp.dot(uh, bim_ref[hf], preferred_element_type=F32)

    for c in range(2 * half_st // S5_CHAN_CHUNK):
        sl = slice(c * S5_CHAN_CHUNK, (c + 1) * S5_CHAN_CHUNK)
        ar = jnp.broadcast_to(ar_ref[:, sl], (nb, S5_CHAN_CHUNK))
        ai = jnp.broadcast_to(ai_ref[:, sl], (nb, S5_CHAN_CHUNK))

        def step(t, carry, sl=sl, ar=ar, ai=ai):
            xr, xi = carry
            r0 = pl.multiple_of(t * nb, nb)
            nr = ar * xr - ai * xi + xr_s[pl.ds(r0, nb), sl]
            ni = ar * xi + ai * xr + xi_s[pl.ds(r0, nb), sl]
            xr_s[pl.ds(r0, nb), sl] = nr
            xi_s[pl.ds(r0, nb), sl] = ni
            return nr, ni

        xr, xi = lax.fori_loop(0, ts, step, (str_s[:, sl], sti_s[:, sl]), unroll=8)
        str_s[:, sl] = xr
        sti_s[:, sl] = xi

    ys = []
    for hf in range(2):
        st = slice(hf * half_st, (hf + 1) * half_st)
        ys.append(jnp.dot(xr_s[:, st].astype(BF16), cre_ref[hf], preferred_element_type=F32)
                  - jnp.dot(xi_s[:, st].astype(BF16), cim_ref[hf], preferred_element_type=F32))
    y = jnp.concatenate(ys, axis=1) + d_ref[...] * u2
    zz = jnp.dot(_gelu(y).astype(BF16), wglu_ref[...], preferred_element_type=F32)
    z = zz[:, :SSM_WIDTH] * jax.nn.sigmoid(zz[:, SSM_WIDTH:])
    z_ref[...] = z.reshape(ts, nb, SSM_WIDTH)


def _s5(u_t, p):
    seq, nb, _ = u_t.shape
    assert nb == SUBLANES, "the scan maps the batch onto the 8 sublanes"
    ts = S5_TIME_TILE
    nst = SSM_GROUPS * SSM_STATE
    return pl.pallas_call(
        _s5_kernel,
        grid=(seq // ts,),
        in_specs=[
            pl.BlockSpec((ts, nb, SSM_WIDTH), lambda i: (i, 0, 0)),
            _const_spec(p["bre"].shape), _const_spec(p["bim"].shape),
            _const_spec(p["cre"].shape), _const_spec(p["cim"].shape),
            _const_spec((1, nst)), _const_spec((1, nst)),
            _const_spec((1, SSM_WIDTH)), _const_spec((SSM_WIDTH, 2 * SSM_WIDTH)),
        ],
        out_specs=pl.BlockSpec((ts, nb, SSM_WIDTH), lambda i: (i, 0, 0)),
        out_shape=jax.ShapeDtypeStruct((seq, nb, SSM_WIDTH), F32),
        scratch_shapes=[
            pltpu.VMEM((ts * nb, nst), F32), pltpu.VMEM((ts * nb, nst), F32),
            pltpu.VMEM((nb, nst), F32), pltpu.VMEM((nb, nst), F32),
        ],
        compiler_params=_cparams("arbitrary"),
        name="s5",
    )(u_t, p["bre"], p["bim"], p["cre"], p["cim"], p["ar"], p["ai"], p["d"], p["wglu"])


def _prep_s5(a_re, a_im, log_dt, b_re, b_im, c_re, c_im, d, w_glu):
    dt = jnp.exp(log_dt)[:, None]
    mag = jnp.exp(a_re * dt)
    ab_re = mag * jnp.cos(a_im * dt)
    ab_im = mag * jnp.sin(a_im * dt)
    den = a_re * a_re + a_im * a_im
    f_re = ((ab_re - 1.0) * a_re + ab_im * a_im) / den
    f_im = (ab_im * a_re - (ab_re - 1.0) * a_im) / den
    fb_re = f_re[:, :, None] * b_re - f_im[:, :, None] * b_im
    fb_im = f_re[:, :, None] * b_im + f_im[:, :, None] * b_re
    gh = SSM_GROUPS // 2
    eye = jnp.eye(gh, dtype=F32)

    def blk_in(w):
        w = w.reshape(2, gh, SSM_STATE, SSM_GROUP)
        return jnp.einsum("kgph,gj->kghjp", w, eye).reshape(2, gh * SSM_GROUP, gh * SSM_STATE).astype(BF16)

    def blk_out(w):
        w = w.reshape(2, gh, SSM_GROUP, SSM_STATE)
        return jnp.einsum("kghp,gj->kgpjh", w, eye).reshape(2, gh * SSM_STATE, gh * SSM_GROUP).astype(BF16)

    return dict(bre=blk_in(fb_re), bim=blk_in(fb_im), cre=blk_out(c_re), cim=blk_out(c_im),
                ar=ab_re.reshape(1, -1), ai=ab_im.reshape(1, -1), d=d.reshape(1, -1),
                wglu=w_glu.astype(BF16))


def _compress_kernel(r_ref, pe_ref, w1_ref, b1_ref, w2_ref, o_ref):
    r = r_ref[0, 0]
    n = r.shape[0]
    half = CMP_STRIDE * HEAD_DIM
    h1 = jnp.dot((r + pe_ref[0, 0:1, :]).astype(BF16), w1_ref[0, :half, :], preferred_element_type=F32)
    h2 = jnp.dot((r + pe_ref[0, 1:2, :]).astype(BF16), w1_ref[0, half:, :], preferred_element_type=F32)
    hid = h1 + pltpu.roll(h2, n - 1, 0) + b1_ref[0]
    out = jnp.dot(_gelu(hid).astype(BF16), w2_ref[0], preferred_element_type=F32)
    rowi = lax.broadcasted_iota(jnp.int32, out.shape, 0)
    o_ref[0, 0] = jnp.where(rowi < n - 1, out, 0.0)


def _compress(r, pe, w1, b1, w2):
    bsz, four, n, width = r.shape
    return pl.pallas_call(
        _compress_kernel,
        grid=(bsz, four),
        in_specs=[
            pl.BlockSpec((1, 1, n, width), lambda b, c: (b, c, 0, 0)),
            pl.BlockSpec((1, 2, width), lambda b, c: (c // N_KV_HEADS, 0, 0)),
            pl.BlockSpec((1, 2 * width, CMP_HIDDEN), lambda b, c: (c // N_KV_HEADS, 0, 0)),
            pl.BlockSpec((1, 1, CMP_HIDDEN), lambda b, c: (c // N_KV_HEADS, 0, 0)),
            pl.BlockSpec((1, CMP_HIDDEN, HEAD_DIM), lambda b, c: (c // N_KV_HEADS, 0, 0)),
        ],
        out_specs=pl.BlockSpec((1, 1, n, HEAD_DIM), lambda b, c: (b, c, 0, 0)),
        out_shape=jax.ShapeDtypeStruct((bsz, four, n, HEAD_DIM), F32),
        compiler_params=_cparams("parallel", "parallel"),
        name="compress",
    )(r, pe, w1, b1, w2)


def _softmax_pv(st, v_tile, m_ref, acc_ref, rs, mask):
    if mask is not None:
        st = jnp.where(mask, st, NEG_INF)
    m_old = m_ref[rs, :]
    m_new = jnp.maximum(m_old, jnp.max(st, axis=1, keepdims=True))
    pt = jnp.exp(st - jnp.concatenate([m_new] * (KEY_TILE // LANES), axis=1))
    acc_ref[rs, :] = jnp.exp(m_old - m_new) * acc_ref[rs, :] + jnp.dot(
        pt.astype(BF16), v_tile, preferred_element_type=F32)
    m_ref[rs, :] = m_new


def _nsa_kernel(q_ref, kc_ref, vc_ref, selm_ref, ks_ref, vs_ref, kw_ref, vw_ref, gate_ref,
                o_ref, lhs_s, m_s, acc_s, mw_s, accw_s, ocmp_s, pg_s, scmp_s, stw_s, sta_s, stb_s):
    tq = q_ref.shape[2]
    rows = GQA_GROUP * tq
    ch = NSA_CHUNK
    nch = rows // ch
    cpq = tq // ch
    n_cmp = kc_ref.shape[3]
    n_blk = selm_ref.shape[1]
    n_wt = WINDOW // KEY_TILE + 1
    qi = pl.program_id(2)
    s0 = qi * tq
    wt0 = jnp.maximum(qi - WINDOW // KEY_TILE, 0)

    lhs_s[:, 0:LANES] = q_ref[0].reshape(rows, LANES) * (HEAD_DIM ** -0.5)

    def t_of(c):
        return s0 + (c % cpq) * ch + lax.broadcasted_iota(jnp.int32, (ch, 1), 0)

    scmp_s[...] = jnp.dot(lhs_s[:, 0:LANES], kc_ref[0, 0], preferred_element_type=F32)
    for i in range(n_wt):
        stw_s[i] = jnp.dot(lhs_s[:, 0:LANES], kw_ref[0, 0, wt0 + i], preferred_element_type=F32)

    cmp_end = lax.broadcasted_iota(jnp.int32, (ch, n_cmp), 1) * CMP_STRIDE + (CMP_BLOCK - 1)
    for c in range(nch):
        rs = slice(c * ch, (c + 1) * ch)
        qs = slice((c % cpq) * ch, (c % cpq + 1) * ch)
        s = scmp_s[rs, :]
        cmask = cmp_end <= t_of(c)
        s = jnp.where(cmask, s, NEG_INF)
        e = jnp.exp(s - jnp.max(s, axis=1, keepdims=True))
        p = jnp.where(cmask, e, 0.0) / jnp.sum(e, axis=1, keepdims=True)
        ocmp_s[rs, :] = jnp.dot(p.astype(BF16), vc_ref[0, 0], preferred_element_type=F32)
        if c < cpq:
            pg_s[qs, :] = p
        else:
            pg_s[qs, :] += p

    pg = pg_s[...]
    selm = selm_ref[...]
    hi = pg.astype(BF16)
    r1 = pg - hi.astype(F32)
    mid = r1.astype(BF16)
    lo = (r1 - mid.astype(F32)).astype(BF16)
    bs = (jnp.dot(hi, selm, preferred_element_type=F32) + jnp.dot(mid, selm, preferred_element_type=F32)
          + jnp.dot(lo, selm, preferred_element_type=F32))
    tq_col = s0 + lax.broadcasted_iota(jnp.int32, (tq, 1), 0)
    j_idx = lax.broadcasted_iota(jnp.int32, (tq, n_blk), 1)
    cur = tq_col >> int(math.log2(SEL_BLOCK))
    forced = (j_idx == 0) | (j_idx == cur) | (j_idx == cur - 1)
    causal_b = j_idx * SEL_BLOCK <= tq_col
    bs = jnp.where(causal_b, bs + jnp.where(forced, FORCE_BONUS, 0.0), NEG_INF)

    work = bs.T
    idx = lax.broadcasted_iota(jnp.int32, (n_blk, tq), 0).astype(F32)
    picked = jnp.zeros((n_blk, tq), F32)
    for _ in range(min(N_SEL, n_blk)):
        mx = jnp.max(work, axis=0, keepdims=True)
        first = jnp.min(jnp.where(work == mx, idx, float(n_blk)), axis=0, keepdims=True)
        hit = idx == first
        picked = jnp.where(hit, 1.0, picked)
        work = jnp.where(hit, -jnp.inf, work)
    sel = (picked.T > 0.0) & causal_b
    bias = jnp.where(sel, 0.0, NEG_INF).astype(BF16)
    for g in range(GQA_GROUP):
        lhs_s[g * tq:(g + 1) * tq, LANES:2 * LANES] = bias

    m_s[...] = jnp.full(m_s.shape, NEG_INF, F32)
    acc_s[...] = jnp.zeros(acc_s.shape, F32)
    mw_s[...] = jnp.full(mw_s.shape, NEG_INF, F32)
    accw_s[...] = jnp.zeros(accw_s.shape, F32)
    key_lane = lax.broadcasted_iota(jnp.int32, (ch, KEY_TILE), 1)

    for i in range(n_wt):
        for c in range(nch):
            rs = slice(c * ch, (c + 1) * ch)
            pos = (wt0 + i) * KEY_TILE + key_lane
            t = t_of(c)
            mask = (pos <= t) & (pos > t - WINDOW)
            _softmax_pv(stw_s[i, rs, :], vw_ref[0, 0, wt0 + i], mw_s, accw_s, rs, mask)

    def produce(st_ref, kt):
        st_ref[...] = jnp.dot(lhs_s[...], ks_ref[0, 0, kt], preferred_element_type=F32)

    def consume(st_ref, kt, diag):
        for c in range(nch):
            rs = slice(c * ch, (c + 1) * ch)
            mask = (kt * KEY_TILE + key_lane <= t_of(c)) if diag else None
            _softmax_pv(st_ref[rs, :], vs_ref[0, 0, kt], m_s, acc_s, rs, mask)

    produce(sta_s, 0)

    def sel_pair(j, carry):
        kt = 2 * j
        produce(stb_s, kt + 1)
        consume(sta_s, kt, False)
        produce(sta_s, kt + 2)
        consume(stb_s, kt + 1, False)
        return carry

    lax.fori_loop(0, qi >> 1, sel_pair, 0)

    @pl.when((qi & 1) == 0)
    def _():
        consume(sta_s, qi, True)

    @pl.when((qi & 1) == 1)
    def _():
        produce(stb_s, qi)
        consume(sta_s, qi - 1, False)
        consume(stb_s, qi, True)

    gate = gate_ref[0, 0]
    outs = []
    for g in range(GQA_GROUP):
        rs = slice(g * tq, (g + 1) * tq)
        a_sel = acc_s[rs, :]
        a_win = accw_s[rs, :]
        outs.append(gate[:, g:g + 1] * ocmp_s[rs, 0:HEAD_DIM]
                    + gate[:, GQA_GROUP + g:GQA_GROUP + g + 1]
                    * (a_sel[:, :HEAD_DIM] / a_sel[:, HEAD_DIM:HEAD_DIM + 1])
                    + gate[:, 2 * GQA_GROUP + g:2 * GQA_GROUP + g + 1]
                    * (a_win[:, :HEAD_DIM] / a_win[:, HEAD_DIM:HEAD_DIM + 1]))
    o_ref[0] = jnp.concatenate(outs, axis=1)


def _nsa(q_r, kc_t, vc_a, selm, ks_a, vs_a, kw_t, vw_a, gate_r):
    bsz, _, seq, _ = q_r.shape
    tq = KEY_TILE
    nt = seq // tq
    n_cmp = kc_t.shape[3]
    rows = GQA_GROUP * tq
    bh = lambda b, h, i: (b, h, 0, 0)
    bh5 = lambda b, h, i: (b, h, 0, 0, 0)
    return pl.pallas_call(
        _nsa_kernel,
        grid=(bsz, N_KV_HEADS, nt),
        in_specs=[
            pl.BlockSpec((1, GQA_GROUP, tq, LANES), lambda b, h, i: (b, h, i, 0)),
            pl.BlockSpec((1, 1, LANES, n_cmp), bh),
            pl.BlockSpec((1, 1, n_cmp, LANES), bh),
            _const_spec(selm.shape),
            pl.BlockSpec((1, 1, nt, 2 * LANES, tq), bh5),
            pl.BlockSpec((1, 1, nt, tq, LANES), bh5),
            pl.BlockSpec((1, 1, nt, LANES, tq), bh5),
            pl.BlockSpec((1, 1, nt, tq, LANES), bh5),
            pl.BlockSpec((1, 1, tq, HEAD_DIM), lambda b, h, i: (b, h, i, 0)),
        ],
        out_specs=pl.BlockSpec((1, tq, GQA_GROUP * HEAD_DIM), lambda b, h, i: (b, i, h)),
        out_shape=jax.ShapeDtypeStruct((bsz, seq, Q_WIDTH), F32),
        scratch_shapes=[
            pltpu.VMEM((rows, 2 * LANES), BF16),
            pltpu.VMEM((rows, LANES), F32), pltpu.VMEM((rows, LANES), F32),
            pltpu.VMEM((rows, LANES), F32), pltpu.VMEM((rows, LANES), F32),
            pltpu.VMEM((rows, LANES), F32),
            pltpu.VMEM((tq, n_cmp), F32),
            pltpu.VMEM((rows, n_cmp), F32),
            pltpu.VMEM((WINDOW // KEY_TILE + 1, rows, KEY_TILE), F32),
            pltpu.VMEM((rows, KEY_TILE), F32), pltpu.VMEM((rows, KEY_TILE), F32),
        ],
        compiler_params=_cparams("parallel", "parallel", "arbitrary"),
        name="nsa",
    )(q_r, kc_t, vc_a, selm, ks_a, vs_a, kw_t, vw_a, gate_r)


def _sel_fold_matrix(n_cmp_pad, n_blk):
    ratio = SEL_BLOCK // CMP_STRIDE
    span = ratio + CMP_BLOCK // CMP_STRIDE - 1
    pad_l = CMP_BLOCK // CMP_STRIDE - 1
    i = np.arange(n_cmp_pad)[:, None]
    j = np.arange(n_blk)[None, :]
    lo = ratio * j - pad_l
    return jnp.asarray(((i >= lo) & (i < lo + span)), dtype=BF16)


def _block_onehot(nt, n_blk):
    key_blk = (np.arange(nt)[:, None] * KEY_TILE + np.arange(KEY_TILE)[None, :]) // SEL_BLOCK
    return jnp.asarray(key_blk[:, None, :] == np.arange(n_blk)[None, :, None], dtype=BF16)


def _heads_first(a, bsz, seq):
    return a.reshape(bsz, seq, -1, HEAD_DIM).transpose(0, 2, 1, 3)


def _key_tiles_t(k, nt, depth):
    b, h, _, hd = k.shape
    kt = k.reshape(b, h, nt, KEY_TILE, hd).transpose(0, 1, 2, 4, 3)
    return jnp.pad(kt, ((0, 0),) * 3 + ((0, depth - hd), (0, 0)))


def _value_tiles(v, nt):
    b, h, s, hd = v.shape
    va = jnp.concatenate([v, jnp.ones((b, h, s, 1), v.dtype), jnp.zeros((b, h, s, LANES - hd - 1), v.dtype)], -1)
    return va.reshape(b, h, nt, KEY_TILE, LANES)


def _merge_kernel(x_ref, z_ref, a_ref, sgs_ref, sgn_ref, wbs_ref, wbn_ref, wo_ref, o_ref):
    y_ssm = jnp.dot(z_ref[...].astype(BF16), wbs_ref[...], preferred_element_type=F32)
    y_nsa = jnp.dot(a_ref[...].astype(BF16), wbn_ref[...], preferred_element_type=F32)
    merged = sgs_ref[...] * y_ssm + sgn_ref[...] * y_nsa
    o_ref[...] = x_ref[...] + jnp.dot(merged.astype(BF16), wo_ref[...], preferred_element_type=F32)


def _merge(x2, z, a, sgs, sgn, wbs, wbn, wo):
    t = x2.shape[0]
    tm = ROW_TILE
    row = lambda i: (i, 0)
    return pl.pallas_call(
        _merge_kernel,
        grid=(t // tm,),
        in_specs=[
            pl.BlockSpec((tm, D_MODEL), row), pl.BlockSpec((tm, SSM_WIDTH), row),
            pl.BlockSpec((tm, Q_WIDTH), row), pl.BlockSpec((tm, D_MODEL), row),
            pl.BlockSpec((tm, D_MODEL), row),
            _const_spec(wbs.shape), _const_spec(wbn.shape), _const_spec(wo.shape),
        ],
        out_specs=pl.BlockSpec((tm, D_MODEL), row),
        out_shape=jax.ShapeDtypeStruct((t, D_MODEL), F32),
        compiler_params=_cparams("parallel"),
        name="merge",
    )(x2, z, a, sgs, sgn, wbs, wbn, wo)


def _ffn_kernel(x_ref, xh_ref, g_ref, wa_ref, wb_ref, cw_ref, cb_ref, wo_ref, gf_ref, o_ref, a_s,
                *, tiles_per_seq, final):
    tm = x_ref.shape[0]
    halo = xh_ref.shape[0]
    first = (pl.program_id(0) % tiles_per_seq) == 0
    x = x_ref[...]
    g = g_ref[...]
    h = _rms(x, g).astype(BF16)
    hh = _rms(xh_ref[...], g).astype(BF16)
    keep_halo = jnp.where(first, 0.0, 1.0)
    acc = jnp.zeros((tm, D_MODEL), F32)
    for c in range(D_FF // FF_CHUNK):
        cs = slice(c * FF_CHUNK, (c + 1) * FF_CHUNK)
        a_s[0:halo, :] = jnp.dot(hh, wa_ref[:, cs], preferred_element_type=F32) * keep_halo
        a_s[halo:halo + tm, :] = jnp.dot(h, wa_ref[:, cs], preferred_element_type=F32)
        b = jnp.dot(h, wb_ref[:, cs], preferred_element_type=F32)
        cw = cw_ref[:, cs]
        conv = cb_ref[:, cs]
        for k in range(CONV_WIDTH):
            off = halo - (CONV_WIDTH - 1) + k
            conv = conv + cw[k:k + 1, :] * a_s[off:off + tm, :]
        act = (_gelu(conv) * b).astype(BF16)
        acc = acc + jnp.dot(act, wo_ref[cs, :], preferred_element_type=F32)
    y = x + acc
    if final:
        y = _rms(y, gf_ref[...])
    o_ref[...] = y


def _ffn(x2, g, wa, wb, cw, cb, wo, gf, seq, final):
    t = x2.shape[0]
    tm = ROW_TILE
    halo = SUBLANES
    tps = seq // tm
    row = lambda i: (i, 0)
    return pl.pallas_call(
        functools.partial(_ffn_kernel, tiles_per_seq=tps, final=final),
        grid=(t // tm,),
        in_specs=[
            pl.BlockSpec((tm, D_MODEL), row),
            pl.BlockSpec((halo, D_MODEL), lambda i: (jnp.maximum(i * (tm // halo) - 1, 0), 0)),
            _const_spec((1, D_MODEL)),
            _const_spec(wa.shape), _const_spec(wb.shape),
            _const_spec(cw.shape), _const_spec(cb.shape), _const_spec(wo.shape),
            _const_spec((1, D_MODEL)),
        ],
        out_specs=pl.BlockSpec((tm, D_MODEL), row),
        out_shape=jax.ShapeDtypeStruct((t, D_MODEL), F32),
        scratch_shapes=[pltpu.VMEM((tm + halo, FF_CHUNK), F32)],
        compiler_params=_cparams("parallel"),
        name="ffn",
    )(x2, x2, g, wa, wb, cw, cb, wo, gf)


def _mixer_layer(x2, bsz, seq, cos, sin, norm_g, w_in, s5p, pe, w1, b1, w2, wbs, wbn, wo, selm, onehot):
    nt = seq // KEY_TILE
    u, q, kvc, kvsw, gate, sgs, sgn = _inproj(x2, norm_g, w_in, cos, sin, seq)

    u_t = u.reshape(bsz, seq, SSM_WIDTH).transpose(1, 0, 2)
    z = _s5(u_t, s5p).transpose(1, 0, 2).reshape(bsz * seq, SSM_WIDTH)

    n_rows = seq // CMP_STRIDE
    r = _heads_first(kvc, bsz, seq).reshape(bsz, 2 * N_KV_HEADS, n_rows, CMP_STRIDE * HEAD_DIM)
    kvc_c = _compress(r, pe, w1, b1, w2)
    kc_t = jnp.pad(kvc_c[:, :N_KV_HEADS].transpose(0, 1, 3, 2).astype(BF16),
                   ((0, 0), (0, 0), (0, LANES - HEAD_DIM), (0, 0)))
    vc_a = jnp.pad(kvc_c[:, N_KV_HEADS:].astype(BF16), ((0, 0), (0, 0), (0, 0), (0, LANES - HEAD_DIM)))
    kv = _heads_first(kvsw, bsz, seq)
    ks_t = _key_tiles_t(kv[:, 0:2], nt, LANES)
    ks_a = jnp.concatenate([ks_t, jnp.broadcast_to(onehot, (bsz, N_KV_HEADS) + onehot.shape)], axis=3)
    kw_t = _key_tiles_t(kv[:, 2:4], nt, LANES)
    vs_a = _value_tiles(kv[:, 4:6], nt)
    vw_a = _value_tiles(kv[:, 6:8], nt)
    q_r = jnp.pad(_heads_first(q, bsz, seq), ((0, 0), (0, 0), (0, 0), (0, LANES - HEAD_DIM)))
    gate_r = _heads_first(gate, bsz, seq)
    a = _nsa(q_r, kc_t, vc_a, selm, ks_a, vs_a, kw_t, vw_a, gate_r).reshape(bsz * seq, Q_WIDTH)

    return _merge(x2, z, a, sgs, sgn, wbs, wbn, wo)


def kernel(x, norm_mix, w_in, ssm_a_re, ssm_a_im, ssm_log_dt, ssm_b_re, ssm_b_im, ssm_c_re, ssm_c_im,
           ssm_d, ssm_w_glu, cmp_pe_k, cmp_w1_k, cmp_b1_k, cmp_w2_k, cmp_pe_v, cmp_w1_v, cmp_b1_v,
           cmp_w2_v, w_branch_ssm, w_branch_nsa, w_out, norm_ffn, w_ffn_in, ffn_conv_w, ffn_conv_b,
           w_ffn_out, norm_final):
    bsz, seq, _ = x.shape
    depth = norm_mix.shape[0]
    assert seq % ROW_TILE == 0 and seq >= (WINDOW // KEY_TILE + 1) * KEY_TILE
    cos, sin = _rope_tables(seq)
    assert N_SEL <= seq // SEL_BLOCK <= LANES, "selection blocks are mapped onto one vreg of lanes"
    selm = _sel_fold_matrix(seq // CMP_STRIDE, LANES)
    onehot = _block_onehot(seq // KEY_TILE, LANES)
    half = CMP_STRIDE * HEAD_DIM
    x2 = x.reshape(bsz * seq, D_MODEL)
    for l in range(depth):
        s5p = _prep_s5(ssm_a_re[l], ssm_a_im[l], ssm_log_dt[l], ssm_b_re[l], ssm_b_im[l],
                       ssm_c_re[l], ssm_c_im[l], ssm_d[l], ssm_w_glu[l])
        pe = jnp.stack([cmp_pe_k[l], cmp_pe_v[l]]).reshape(2, 2, half)
        w1 = jnp.stack([cmp_w1_k[l], cmp_w1_v[l]]).astype(BF16)
        b1 = jnp.stack([cmp_b1_k[l], cmp_b1_v[l]]).reshape(2, 1, CMP_HIDDEN)
        w2 = jnp.stack([cmp_w2_k[l], cmp_w2_v[l]]).astype(BF16)
        x2 = _mixer_layer(x2, bsz, seq, cos, sin, norm_mix[l].reshape(1, -1), _prep_w_in(w_in[l]), s5p,
                          pe, w1, b1, w2, w_branch_ssm[l].astype(BF16), w_branch_nsa[l].astype(BF16),
                          w_out[l].astype(BF16), selm, onehot)
        wf = w_ffn_in[l].astype(BF16)
        x2 = _ffn(x2, norm_ffn[l].reshape(1, -1), wf[:, :D_FF], wf[:, D_FF:], ffn_conv_w[l],
                  ffn_conv_b[l].reshape(1, -1), w_ffn_out[l].astype(BF16), norm_final.reshape(1, -1),
                  seq, final=(l == depth - 1))
    return x2.reshape(bsz, seq, D_MODEL)
```

```python
import functools
import math

import numpy as np
import jax
import jax.numpy as jnp
from jax import lax
from jax.experimental import pallas as pl
from jax.experimental.pallas import tpu as pltpu

F32 = jnp.float32
BF16 = jnp.bfloat16

D_MODEL = 1024
SSM_WIDTH = 512
SSM_GROUP = 16
SSM_GROUPS = 32
SSM_STATE = 64
N_Q_HEADS = 8
N_KV_HEADS = 2
HEAD_DIM = 64
GQA_GROUP = 4
CMP_BLOCK = 32
CMP_STRIDE = 16
CMP_HIDDEN = 256
SEL_BLOCK = 64
N_SEL = 16
WINDOW = 512
N_NSA_BRANCH = 3
D_FF = 2816
CONV_WIDTH = 3
RMS_EPS = 1e-6
NEG_INF = -1e30
FORCED = 3
ROPE_THETA = 10000.0
Q_WIDTH = N_Q_HEADS * HEAD_DIM
KV_WIDTH = N_KV_HEADS * HEAD_DIM

LANES = 128
SUBLANES = 8
MXU_DIM = 256
VMEM_LIMIT = 56 * 1024 * 1024

ROW_TILE = 512
S5_TIME_TILE = 64
S5_CHAN_CHUNK = 1024
KEY_TILE = MXU_DIM
FF_CHUNK = MXU_DIM
Q_SCALE = HEAD_DIM ** -0.5 * math.log2(math.e)
V_ROWS = 80

_C_U = 0
_C_Q = _C_U + SSM_WIDTH
_C_KCMP = _C_Q + Q_WIDTH
_C_VCMP = _C_KCMP + KV_WIDTH
_C_KSEL = _C_VCMP + KV_WIDTH
_C_KWIN = _C_KSEL + KV_WIDTH
_C_VSEL = _C_KWIN + KV_WIDTH
_C_VWIN = _C_VSEL + KV_WIDTH
_C_GATE = _C_VWIN + KV_WIDTH
_C_GSSM = _C_GATE + LANES
_C_GNSA = _C_GSSM + D_MODEL
IN_WIDTH_PAD = _C_GNSA + D_MODEL


def _cparams(*sem):
    return pltpu.CompilerParams(dimension_semantics=sem, vmem_limit_bytes=VMEM_LIMIT)


def _gelu(x):
    return x * (0.5 * (1.0 + jnp.tanh(math.sqrt(2.0 / math.pi) * (x + 0.044715 * (x * x * x)))))


def _rms(x, g):
    ms = jnp.mean(x * x, axis=-1, keepdims=True)
    return (x * lax.rsqrt(ms + RMS_EPS)) * g


def _const_spec(shape):
    nd = len(shape)
    return pl.BlockSpec(shape, lambda *_: (0,) * nd)


def _inproj_kernel(x_ref, g_ref, w_ref, cos_ref, sin_ref,
                   u_ref, qt_ref, kvc_ref, ks_ref, kw_ref, vst_ref, vwt_ref, gatet_ref, sgs_ref, sgn_ref,
                   *, tiles_per_seq):
    tm = x_ref.shape[0]
    tk = KEY_TILE
    n_sub = tm // tk
    tile0 = (pl.program_id(0) % tiles_per_seq) * n_sub
    h = _rms(x_ref[...], g_ref[...]).astype(BF16)
    lane_k = lax.broadcasted_iota(jnp.int32, (tk, LANES), 1)
    row_k = lax.broadcasted_iota(jnp.int32, (tk, LANES), 0)

    def split_heads(a, j):
        piece = a[j * tk:(j + 1) * tk, :]
        return (jnp.where(lane_k < HEAD_DIM, piece, 0.0),
                jnp.where(lane_k < HEAD_DIM, pltpu.roll(piece, HEAD_DIM, 1), 0.0))

    def put_values_t(v, out_ref):
        vt = v.T.astype(BF16)
        tail = jnp.where(lax.broadcasted_iota(jnp.int32, (V_ROWS - HEAD_DIM, tk), 0) == 0, 1.0, 0.0)
        for hh in range(N_KV_HEADS):
            for j in range(n_sub):
                out_ref[0, hh, j, 0:HEAD_DIM, :] = vt[hh * HEAD_DIM:(hh + 1) * HEAD_DIM, j * tk:(j + 1) * tk]
                out_ref[0, hh, j, HEAD_DIM:V_ROWS, :] = tail.astype(BF16)

    def proj(c0, n):
        return jnp.dot(h, w_ref[:, c0:c0 + n], preferred_element_type=F32)

    cos = cos_ref[...]
    sin = sin_ref[...]
    lane = lax.broadcasted_iota(jnp.int32, (tm, LANES), 1)
    first_half = (lane & (HEAD_DIM // 2)) == 0

    def rope(v):
        swapped = jnp.where(first_half, pltpu.roll(v, LANES - HEAD_DIM // 2, 1),
                            pltpu.roll(v, HEAD_DIM // 2, 1))
        return v * cos + swapped * sin

    u_ref[...] = proj(_C_U, SSM_WIDTH)

    qt_ref[0, :, :, HEAD_DIM:LANES, :] = jnp.zeros((N_KV_HEADS, n_sub, LANES - HEAD_DIM, GQA_GROUP * tk), BF16)
    for c in range(Q_WIDTH // LANES):
        qt = (rope(proj(_C_Q + c * LANES, LANES)) * Q_SCALE).T.astype(BF16)
        for hl in range(2):
            head = 2 * c + hl
            hkv, g = head // GQA_GROUP, head % GQA_GROUP
            for j in range(n_sub):
                qt_ref[0, hkv, j, 0:HEAD_DIM, g * tk:(g + 1) * tk] = (
                    qt[hl * HEAD_DIM:(hl + 1) * HEAD_DIM, j * tk:(j + 1) * tk])

    kvc_ref[:, 0:LANES] = rope(proj(_C_KCMP, LANES))
    kvc_ref[:, LANES:2 * LANES] = proj(_C_VCMP, LANES)

    k_sel = rope(proj(_C_KSEL, LANES))
    k_win = rope(proj(_C_KWIN, LANES))
    for j in range(n_sub):
        onehot = jnp.where(((tile0 + j) * tk + row_k) >> int(math.log2(SEL_BLOCK)) == lane_k, 1.0, 0.0)
        for hh, (ks_h, kw_h) in enumerate(zip(split_heads(k_sel, j), split_heads(k_win, j))):
            ks_ref[0, hh, j, :, 0:LANES] = ks_h.astype(BF16)
            ks_ref[0, hh, j, :, LANES:2 * LANES] = onehot.astype(BF16)
            kw_ref[0, hh, j] = kw_h.astype(BF16)
    put_values_t(proj(_C_VSEL, LANES), vst_ref)
    put_values_t(proj(_C_VWIN, LANES), vwt_ref)

    gate_t = jax.nn.sigmoid(proj(_C_GATE, LANES)).T
    for hh in range(N_KV_HEADS):
        gatet_ref[0, hh] = gate_t[hh * HEAD_DIM:hh * HEAD_DIM + 2 * SUBLANES, :]

    for c in range(D_MODEL // MXU_DIM):
        sgs_ref[:, c * MXU_DIM:(c + 1) * MXU_DIM] = jax.nn.sigmoid(proj(_C_GSSM + c * MXU_DIM, MXU_DIM))
        sgn_ref[:, c * MXU_DIM:(c + 1) * MXU_DIM] = jax.nn.sigmoid(proj(_C_GNSA + c * MXU_DIM, MXU_DIM))


def _inproj(x2, g, w, cos, sin, seq):
    t = x2.shape[0]
    bsz = t // seq
    tm = ROW_TILE
    spt = seq // tm
    tk = KEY_TILE
    n_sub = tm // tk
    nt = seq // tk
    row = lambda i: (i, 0)
    per_tile = lambda i: (i // spt, 0, i % spt, 0, 0)
    rows_out = lambda n, dt: (jax.ShapeDtypeStruct((t, n), dt), pl.BlockSpec((tm, n), row))
    tile_out = lambda r, c: (jax.ShapeDtypeStruct((bsz, N_KV_HEADS, nt, r, c), BF16),
                             pl.BlockSpec((1, N_KV_HEADS, n_sub, r, c), per_tile))
    outs, out_specs = zip(
        rows_out(SSM_WIDTH, F32),
        tile_out(LANES, GQA_GROUP * tk),
        rows_out(2 * KV_WIDTH, F32),
        tile_out(tk, 2 * LANES),
        tile_out(tk, LANES),
        tile_out(V_ROWS, tk),
        tile_out(V_ROWS, tk),
        (jax.ShapeDtypeStruct((bsz, N_KV_HEADS, 2 * SUBLANES, seq), F32),
         pl.BlockSpec((1, N_KV_HEADS, 2 * SUBLANES, tm), lambda i: (i // spt, 0, 0, i % spt))),
        rows_out(D_MODEL, F32),
        rows_out(D_MODEL, F32),
    )
    return pl.pallas_call(
        functools.partial(_inproj_kernel, tiles_per_seq=spt),
        grid=(t // tm,),
        in_specs=[
            pl.BlockSpec((tm, D_MODEL), row),
            _const_spec((1, D_MODEL)),
            _const_spec((D_MODEL, IN_WIDTH_PAD)),
            pl.BlockSpec((tm, LANES), lambda i: (i % spt, 0)),
            pl.BlockSpec((tm, LANES), lambda i: (i % spt, 0)),
        ],
        out_specs=list(out_specs),
        out_shape=list(outs),
        compiler_params=_cparams("parallel"),
        name="inproj",
    )(x2, g, w, cos, sin)


def _prep_w_in(w_in):
    o = np.cumsum((0, SSM_WIDTH, Q_WIDTH) + (KV_WIDTH,) * 6 + (N_Q_HEADS * N_NSA_BRANCH, D_MODEL, D_MODEL))
    u, q, k_cmp, v_cmp, k_sel, v_sel, k_win, v_win, gate, g_ssm, g_nsa = (
        w_in[:, o[i]:o[i + 1]] for i in range(11))
    gate = gate.reshape(D_MODEL, N_KV_HEADS, GQA_GROUP, N_NSA_BRANCH).transpose(0, 1, 3, 2)
    gate = gate.reshape(D_MODEL, N_KV_HEADS, N_NSA_BRANCH * GQA_GROUP)
    gate = jnp.pad(gate, ((0, 0), (0, 0), (0, HEAD_DIM - N_NSA_BRANCH * GQA_GROUP))).reshape(D_MODEL, LANES)
    w = jnp.concatenate([u, q, k_cmp, v_cmp, k_sel, k_win, v_sel, v_win, gate, g_ssm, g_nsa], axis=1)
    return w.astype(BF16)


def _rope_tables(seq):
    inv_freq = ROPE_THETA ** (-jnp.arange(0, HEAD_DIM, 2, dtype=F32) / HEAD_DIM)
    ang = jnp.arange(seq, dtype=F32)[:, None] * inv_freq[None, :]
    cos, sin = jnp.cos(ang), jnp.sin(ang)
    return jnp.tile(cos, (1, 4)), jnp.tile(jnp.concatenate([-sin, sin], axis=1), (1, 2))


def _s5_kernel(u_ref, bre_ref, bim_ref, cre_ref, cim_ref, ar_ref, ai_ref, d_ref, wglu_ref,
               z_ref, xr_s, xi_s, str_s, sti_s, io_s):
    nb, ts, _ = u_ref.shape
    rows = ts * nb
    half_in = SSM_WIDTH // 2
    half_st = SSM_GROUPS * SSM_STATE // 2

    @pl.when(pl.program_id(0) == 0)
    def _():
        str_s[...] = jnp.zeros_like(str_s)
        sti_s[...] = jnp.zeros_like(sti_s)

    n_lc = SSM_WIDTH // LANES
    for b in range(nb):
        for lc in range(n_lc):
            io_s[lc, pl.ds(b, ts, stride=nb), :] = u_ref[b, :, lc * LANES:(lc + 1) * LANES]
    u2 = jnp.concatenate([io_s[lc] for lc in range(n_lc)], axis=1)
    ub = u2.astype(BF16)
    for hf in range(2):
        uh = ub[:, hf * half_in:(hf + 1) * half_in]
        xr_s[:, hf * half_st:(hf + 1) * half_st] = jnp.dot(uh, bre_ref[hf], preferred_element_type=F32)
        xi_s[:, hf * half_st:(hf + 1) * half_st] = jnp.dot(uh, bim_ref[hf], preferred_element_type=F32)

    for c in range(2 * half_st // S5_CHAN_CHUNK):
        sl = slice(c * S5_CHAN_CHUNK, (c + 1) * S5_CHAN_CHUNK)
        ar = jnp.broadcast_to(ar_ref[:, sl], (nb, S5_CHAN_CHUNK))
        ai = jnp.broadcast_to(ai_ref[:, sl], (nb, S5_CHAN_CHUNK))

        def step(t, carry, sl=sl, ar=ar, ai=ai):
            xr, xi = carry
            r0 = pl.multiple_of(t * nb, nb)
            nr = ar * xr - ai * xi + xr_s[pl.ds(r0, nb), sl]
            ni = ar * xi + ai * xr + xi_s[pl.ds(r0, nb), sl]
            xr_s[pl.ds(r0, nb), sl] = nr
            xi_s[pl.ds(r0, nb), sl] = ni
            return nr, ni

        xr, xi = lax.fori_loop(0, ts, step, (str_s[:, sl], sti_s[:, sl]), unroll=8)
        str_s[:, sl] = xr
        sti_s[:, sl] = xi

    ys = []
    for hf in range(2):
        st = slice(hf * half_st, (hf + 1) * half_st)
        ys.append(jnp.dot(xr_s[:, st].astype(BF16), cre_ref[hf], preferred_element_type=F32)
                  - jnp.dot(xi_s[:, st].astype(BF16), cim_ref[hf], preferred_element_type=F32))
    y = jnp.concatenate(ys, axis=1) + d_ref[...] * u2
    zz = jnp.dot(_gelu(y).astype(BF16), wglu_ref[...], preferred_element_type=F32)
    z = zz[:, :SSM_WIDTH] * jax.nn.sigmoid(zz[:, SSM_WIDTH:])
    for lc in range(n_lc):
        io_s[lc] = z[:, lc * LANES:(lc + 1) * LANES]
    for b in range(nb):
        for lc in range(n_lc):
            z_ref[b, :, lc * LANES:(lc + 1) * LANES] = io_s[lc, pl.ds(b, ts, stride=nb), :]


def _s5(u, p):
    nb, seq, _ = u.shape
    assert nb == SUBLANES, "the scan maps the batch onto the 8 sublanes"
    ts = S5_TIME_TILE
    nst = SSM_GROUPS * SSM_STATE
    return pl.pallas_call(
        _s5_kernel,
        grid=(seq // ts,),
        in_specs=[
            pl.BlockSpec((nb, ts, SSM_WIDTH), lambda i: (0, i, 0)),
            _const_spec(p["bre"].shape), _const_spec(p["bim"].shape),
            _const_spec(p["cre"].shape), _const_spec(p["cim"].shape),
            _const_spec((1, nst)), _const_spec((1, nst)),
            _const_spec((1, SSM_WIDTH)), _const_spec((SSM_WIDTH, 2 * SSM_WIDTH)),
        ],
        out_specs=pl.BlockSpec((nb, ts, SSM_WIDTH), lambda i: (0, i, 0)),
        out_shape=jax.ShapeDtypeStruct((nb, seq, SSM_WIDTH), F32),
        scratch_shapes=[
            pltpu.VMEM((ts * nb, nst), F32), pltpu.VMEM((ts * nb, nst), F32),
            pltpu.VMEM((nb, nst), F32), pltpu.VMEM((nb, nst), F32),
            pltpu.VMEM((SSM_WIDTH // LANES, ts * nb, LANES), F32),
        ],
        compiler_params=_cparams("arbitrary"),
        name="s5",
    )(u, p["bre"], p["bim"], p["cre"], p["cim"], p["ar"], p["ai"], p["d"], p["wglu"])


def _prep_s5(a_re, a_im, log_dt, b_re, b_im, c_re, c_im, d, w_glu):
    dt = jnp.exp(log_dt)[:, None]
    mag = jnp.exp(a_re * dt)
    ab_re = mag * jnp.cos(a_im * dt)
    ab_im = mag * jnp.sin(a_im * dt)
    den = a_re * a_re + a_im * a_im
    f_re = ((ab_re - 1.0) * a_re + ab_im * a_im) / den
    f_im = (ab_im * a_re - (ab_re - 1.0) * a_im) / den
    fb_re = f_re[:, :, None] * b_re - f_im[:, :, None] * b_im
    fb_im = f_re[:, :, None] * b_im + f_im[:, :, None] * b_re
    gh = SSM_GROUPS // 2
    eye = jnp.eye(gh, dtype=F32)

    def blk_in(w):
        w = w.reshape(2, gh, SSM_STATE, SSM_GROUP)
        return jnp.einsum("kgph,gj->kghjp", w, eye).reshape(2, gh * SSM_GROUP, gh * SSM_STATE).astype(BF16)

    def blk_out(w):
        w = w.reshape(2, gh, SSM_GROUP, SSM_STATE)
        return jnp.einsum("kghp,gj->kgpjh", w, eye).reshape(2, gh * SSM_STATE, gh * SSM_GROUP).astype(BF16)

    return dict(bre=blk_in(fb_re), bim=blk_in(fb_im), cre=blk_out(c_re), cim=blk_out(c_im),
                ar=ab_re.reshape(1, -1), ai=ab_im.reshape(1, -1), d=d.reshape(1, -1),
                wglu=w_glu.astype(BF16))


def _compress_kernel(r_ref, pe_ref, w1_ref, b1_ref, w2_ref, o_ref):
    r = r_ref[0, 0]
    n = r.shape[0]
    half = CMP_STRIDE * HEAD_DIM
    h1 = jnp.dot((r + pe_ref[0, 0:1, :]).astype(BF16), w1_ref[0, :half, :], preferred_element_type=F32)
    h2 = jnp.dot((r + pe_ref[0, 1:2, :]).astype(BF16), w1_ref[0, half:, :], preferred_element_type=F32)
    hid = h1 + pltpu.roll(h2, n - 1, 0) + b1_ref[0]
    out = jnp.dot(_gelu(hid).astype(BF16), w2_ref[0], preferred_element_type=F32)
    rowi = lax.broadcasted_iota(jnp.int32, out.shape, 0)
    o_ref[0, 0] = jnp.where(rowi < n - 1, out, 0.0)


def _compress(r, pe, w1, b1, w2):
    bsz, four, n, width = r.shape
    return pl.pallas_call(
        _compress_kernel,
        grid=(bsz, four),
        in_specs=[
            pl.BlockSpec((1, 1, n, width), lambda b, c: (b, c, 0, 0)),
            pl.BlockSpec((1, 2, width), lambda b, c: (c // N_KV_HEADS, 0, 0)),
            pl.BlockSpec((1, 2 * width, CMP_HIDDEN), lambda b, c: (c // N_KV_HEADS, 0, 0)),
            pl.BlockSpec((1, 1, CMP_HIDDEN), lambda b, c: (c // N_KV_HEADS, 0, 0)),
            pl.BlockSpec((1, CMP_HIDDEN, HEAD_DIM), lambda b, c: (c // N_KV_HEADS, 0, 0)),
        ],
        out_specs=pl.BlockSpec((1, 1, n, HEAD_DIM), lambda b, c: (b, c, 0, 0)),
        out_shape=jax.ShapeDtypeStruct((bsz, four, n, HEAD_DIM), F32),
        compiler_params=_cparams("parallel", "parallel"),
        name="compress",
    )(r, pe, w1, b1, w2)


def _sel_fold_matrix(n_cmp_pad, n_blk):
    ratio = SEL_BLOCK // CMP_STRIDE
    span = ratio + CMP_BLOCK // CMP_STRIDE - 1
    pad_l = CMP_BLOCK // CMP_STRIDE - 1
    i = np.arange(n_cmp_pad)[:, None]
    j = np.arange(n_blk)[None, :]
    lo = ratio * j - pad_l
    return jnp.asarray(((i >= lo) & (i < lo + span)), dtype=BF16)


def _heads_first(a, bsz, seq):
    return a.reshape(bsz, seq, -1, HEAD_DIM).transpose(0, 2, 1, 3)


def _softmax_pv_t(st, vt_tile, m_ref, acc_ref, cs, mask):
    if mask is not None:
        st = jnp.where(mask, st, NEG_INF)
    m_old = m_ref[:, cs]
    m_new = jnp.maximum(m_old, jnp.max(st, axis=0, keepdims=True))
    pt = jnp.exp2(st - m_new).astype(BF16)
    acc_ref[:, cs] = jnp.exp2(m_old - m_new) * acc_ref[:, cs] + jnp.dot(
        vt_tile, pt, preferred_element_type=F32)
    m_ref[:, cs] = m_new


def _nsa_t_kernel(qt_ref, kc_ref, vct_ref, selmt_ref, ks_ref, vst_ref, kw_ref, vwt_ref, gatet_ref,
                  o_ref, lhs_s, m_s, acc_s, mw_s, accw_s, ocmp_s, pg_s, scmp_s, stw_s, sta_s, stb_s):
    cols = qt_ref.shape[4]
    tq = cols // GQA_GROUP
    n_cmp = kc_ref.shape[2]
    n_blk = selmt_ref.shape[0]
    n_wt = WINDOW // KEY_TILE + 1
    qi = pl.program_id(2)
    s0 = qi * tq
    wt0 = jnp.maximum(qi - WINDOW // KEY_TILE, 0)
    heads = [slice(g * tq, (g + 1) * tq) for g in range(GQA_GROUP)]

    lhs_s[0:LANES, :] = qt_ref[0, 0, 0]
    t_row = s0 + lax.broadcasted_iota(jnp.int32, (1, tq), 1)

    scmp_s[...] = jnp.dot(kc_ref[0, 0], lhs_s[0:LANES, :], preferred_element_type=F32)
    for i in range(n_wt):
        stw_s[i] = jnp.dot(kw_ref[0, 0, wt0 + i], lhs_s[0:LANES, :], preferred_element_type=F32)

    cmp_end = lax.broadcasted_iota(jnp.int32, (n_cmp, 1), 0) * CMP_STRIDE + (CMP_BLOCK - 1)
    cmask = cmp_end <= t_row
    for g, cs in enumerate(heads):
        s = jnp.where(cmask, scmp_s[:, cs], NEG_INF)
        e = jnp.exp2(s - jnp.max(s, axis=0, keepdims=True))
        p = jnp.where(cmask, e, 0.0) / jnp.sum(e, axis=0, keepdims=True)
        ocmp_s[:, cs] = jnp.dot(vct_ref[0, 0], p.astype(BF16), preferred_element_type=F32)
        if g == 0:
            pg_s[...] = p
        else:
            pg_s[...] += p

    pg = pg_s[...]
    selmt = selmt_ref[...]
    hi = pg.astype(BF16)
    r1 = pg - hi.astype(F32)
    mid = r1.astype(BF16)
    lo = (r1 - mid.astype(F32)).astype(BF16)
    bs = (jnp.dot(selmt, hi, preferred_element_type=F32) + jnp.dot(selmt, mid, preferred_element_type=F32)
          + jnp.dot(selmt, lo, preferred_element_type=F32))
    j_idx = lax.broadcasted_iota(jnp.int32, (n_blk, 1), 0)
    cur = t_row >> int(math.log2(SEL_BLOCK))
    forced = (j_idx == 0) | (j_idx == cur) | (j_idx == cur - 1)
    causal_b = j_idx * SEL_BLOCK <= t_row
    work = jnp.where(causal_b & ~forced, bs, NEG_INF)
    idx = lax.broadcasted_iota(jnp.int32, (n_blk, tq), 0).astype(F32)
    picked = jnp.where(forced, 1.0, 0.0)
    for _ in range(min(N_SEL, n_blk) - FORCED):
        mx = jnp.max(work, axis=0, keepdims=True)
        first = jnp.min(jnp.where(work == mx, idx, float(n_blk)), axis=0, keepdims=True)
        hit = idx == first
        picked = jnp.where(hit, 1.0, picked)
        work = jnp.where(hit, -jnp.inf, work)
    bias = jnp.where((picked > 0.0) & causal_b, 0.0, NEG_INF).astype(BF16)
    for cs in heads:
        lhs_s[LANES:2 * LANES, cs] = bias

    m_s[...] = jnp.full(m_s.shape, NEG_INF, F32)
    acc_s[...] = jnp.zeros(acc_s.shape, F32)
    mw_s[...] = jnp.full(mw_s.shape, NEG_INF, F32)
    accw_s[...] = jnp.zeros(accw_s.shape, F32)
    key_sub = lax.broadcasted_iota(jnp.int32, (KEY_TILE, 1), 0)

    for i in range(n_wt):
        pos = (wt0 + i) * KEY_TILE + key_sub
        mask = (pos <= t_row) & (pos > t_row - WINDOW)
        for cs in heads:
            _softmax_pv_t(stw_s[i, :, cs], vwt_ref[0, 0, wt0 + i], mw_s, accw_s, cs, mask)

    def produce(st_ref, kt):
        st_ref[...] = jnp.dot(ks_ref[0, 0, kt], lhs_s[...], preferred_element_type=F32)

    def consume(st_ref, kt, diag):
        mask = (kt * KEY_TILE + key_sub <= t_row) if diag else None
        for cs in heads:
            _softmax_pv_t(st_ref[:, cs], vst_ref[0, 0, kt], m_s, acc_s, cs, mask)

    produce(sta_s, 0)

    def sel_pair(j, carry):
        kt = 2 * j
        produce(stb_s, kt + 1)
        consume(sta_s, kt, False)
        produce(sta_s, kt + 2)
        consume(stb_s, kt + 1, False)
        return carry

    lax.fori_loop(0, qi >> 1, sel_pair, 0)

    @pl.when((qi & 1) == 0)
    def _():
        consume(sta_s, qi, True)

    @pl.when((qi & 1) == 1)
    def _():
        produce(stb_s, qi)
        consume(sta_s, qi - 1, False)
        consume(stb_s, qi, True)

    gt = gatet_ref[0, 0]
    outs = []
    for g, cs in enumerate(heads):
        a_sel = acc_s[:, cs]
        a_win = accw_s[:, cs]
        o_t = (gt[g:g + 1, :] * ocmp_s[0:HEAD_DIM, cs]
               + gt[GQA_GROUP + g:GQA_GROUP + g + 1, :]
               * (a_sel[0:HEAD_DIM, :] / a_sel[HEAD_DIM:HEAD_DIM + 1, :])
               + gt[2 * GQA_GROUP + g:2 * GQA_GROUP + g + 1, :]
               * (a_win[0:HEAD_DIM, :] / a_win[HEAD_DIM:HEAD_DIM + 1, :]))
        outs.append(o_t.T)
    o_ref[0] = jnp.concatenate(outs, axis=1)


def _nsa_t(qt, kc, vct, selmt, ks_a, vst, kw, vwt, gatet):
    bsz, _, nt, _, cols = qt.shape
    tq = cols // GQA_GROUP
    n_cmp = kc.shape[2]
    bh4 = lambda b, h, i: (b, h, 0, 0)
    bh5 = lambda b, h, i: (b, h, 0, 0, 0)
    return pl.pallas_call(
        _nsa_t_kernel,
        grid=(bsz, N_KV_HEADS, nt),
        in_specs=[
            pl.BlockSpec((1, 1, 1, LANES, cols), lambda b, h, i: (b, h, i, 0, 0)),
            pl.BlockSpec((1, 1, n_cmp, LANES), bh4),
            pl.BlockSpec((1, 1, LANES, n_cmp), bh4),
            _const_spec(selmt.shape),
            pl.BlockSpec((1, 1, nt, KEY_TILE, 2 * LANES), bh5),
            pl.BlockSpec((1, 1, nt, V_ROWS, KEY_TILE), bh5),
            pl.BlockSpec((1, 1, nt, KEY_TILE, LANES), bh5),
            pl.BlockSpec((1, 1, nt, V_ROWS, KEY_TILE), bh5),
            pl.BlockSpec((1, 1, 2 * SUBLANES, tq), lambda b, h, i: (b, h, 0, i)),
        ],
        out_specs=pl.BlockSpec((1, tq, GQA_GROUP * HEAD_DIM), lambda b, h, i: (b, i, h)),
        out_shape=jax.ShapeDtypeStruct((bsz, nt * tq, Q_WIDTH), F32),
        scratch_shapes=[
            pltpu.VMEM((2 * LANES, cols), BF16),
            pltpu.VMEM((1, cols), F32), pltpu.VMEM((V_ROWS, cols), F32),
            pltpu.VMEM((1, cols), F32), pltpu.VMEM((V_ROWS, cols), F32),
            pltpu.VMEM((LANES, cols), F32),
            pltpu.VMEM((n_cmp, tq), F32),
            pltpu.VMEM((n_cmp, cols), F32),
            pltpu.VMEM((WINDOW // KEY_TILE + 1, KEY_TILE, cols), F32),
            pltpu.VMEM((KEY_TILE, cols), F32), pltpu.VMEM((KEY_TILE, cols), F32),
        ],
        compiler_params=_cparams("parallel", "parallel", "arbitrary"),
        name="nsa",
    )(qt, kc, vct, selmt, ks_a, vst, kw, vwt, gatet)


def _merge_kernel(x_ref, z_ref, a_ref, sgs_ref, sgn_ref, wbs_ref, wbn_ref, wo_ref, o_ref):
    y_ssm = jnp.dot(z_ref[...].astype(BF16), wbs_ref[...], preferred_element_type=F32)
    y_nsa = jnp.dot(a_ref[...].astype(BF16), wbn_ref[...], preferred_element_type=F32)
    merged = sgs_ref[...] * y_ssm + sgn_ref[...] * y_nsa
    o_ref[...] = x_ref[...] + jnp.dot(merged.astype(BF16), wo_ref[...], preferred_element_type=F32)


def _merge(x2, z, a, sgs, sgn, wbs, wbn, wo):
    t = x2.shape[0]
    tm = ROW_TILE
    row = lambda i: (i, 0)
    return pl.pallas_call(
        _merge_kernel,
        grid=(t // tm,),
        in_specs=[
            pl.BlockSpec((tm, D_MODEL), row), pl.BlockSpec((tm, SSM_WIDTH), row),
            pl.BlockSpec((tm, Q_WIDTH), row), pl.BlockSpec((tm, D_MODEL), row),
            pl.BlockSpec((tm, D_MODEL), row),
            _const_spec(wbs.shape), _const_spec(wbn.shape), _const_spec(wo.shape),
        ],
        out_specs=pl.BlockSpec((tm, D_MODEL), row),
        out_shape=jax.ShapeDtypeStruct((t, D_MODEL), F32),
        compiler_params=_cparams("parallel"),
        name="merge",
    )(x2, z, a, sgs, sgn, wbs, wbn, wo)


def _ffn_kernel(x_ref, xh_ref, g_ref, wa_ref, wb_ref, cw_ref, cb_ref, wo_ref, gf_ref, o_ref, a_s,
                *, tiles_per_seq, final):
    tm = x_ref.shape[0]
    halo = xh_ref.shape[0]
    first = (pl.program_id(0) % tiles_per_seq) == 0
    x = x_ref[...]
    g = g_ref[...]
    h = _rms(x, g).astype(BF16)
    hh = _rms(xh_ref[...], g).astype(BF16)
    keep_halo = jnp.where(first, 0.0, 1.0)
    acc = jnp.zeros((tm, D_MODEL), F32)
    for c in range(D_FF // FF_CHUNK):
        cs = slice(c * FF_CHUNK, (c + 1) * FF_CHUNK)
        a_s[0:halo, :] = jnp.dot(hh, wa_ref[:, cs], preferred_element_type=F32) * keep_halo
        a_s[halo:halo + tm, :] = jnp.dot(h, wa_ref[:, cs], preferred_element_type=F32)
        b = jnp.dot(h, wb_ref[:, cs], preferred_element_type=F32)
        cw = cw_ref[:, cs]
        conv = cb_ref[:, cs]
        for k in range(CONV_WIDTH):
            off = halo - (CONV_WIDTH - 1) + k
            conv = conv + cw[k:k + 1, :] * a_s[off:off + tm, :]
        act = (_gelu(conv) * b).astype(BF16)
        acc = acc + jnp.dot(act, wo_ref[cs, :], preferred_element_type=F32)
    y = x + acc
    if final:
        y = _rms(y, gf_ref[...])
    o_ref[...] = y


def _ffn(x2, g, wa, wb, cw, cb, wo, gf, seq, final):
    t = x2.shape[0]
    tm = ROW_TILE
    halo = SUBLANES
    tps = seq // tm
    row = lambda i: (i, 0)
    return pl.pallas_call(
        functools.partial(_ffn_kernel, tiles_per_seq=tps, final=final),
        grid=(t // tm,),
        in_specs=[
            pl.BlockSpec((tm, D_MODEL), row),
            pl.BlockSpec((halo, D_MODEL), lambda i: (jnp.maximum(i * (tm // halo) - 1, 0), 0)),
            _const_spec((1, D_MODEL)),
            _const_spec(wa.shape), _const_spec(wb.shape),
            _const_spec(cw.shape), _const_spec(cb.shape), _const_spec(wo.shape),
            _const_spec((1, D_MODEL)),
        ],
        out_specs=pl.BlockSpec((tm, D_MODEL), row),
        out_shape=jax.ShapeDtypeStruct((t, D_MODEL), F32),
        scratch_shapes=[pltpu.VMEM((tm + halo, FF_CHUNK), F32)],
        compiler_params=_cparams("parallel"),
        name="ffn",
    )(x2, x2, g, wa, wb, cw, cb, wo, gf)


def _mixer_layer(x2, bsz, seq, cos, sin, norm_g, w_in, s5p, pe, w1, b1, w2, wbs, wbn, wo, selmt):
    u, qt, kvc, ks_a, kw, vst, vwt, gatet, sgs, sgn = _inproj(x2, norm_g, w_in, cos, sin, seq)

    z = _s5(u.reshape(bsz, seq, SSM_WIDTH), s5p).reshape(bsz * seq, SSM_WIDTH)

    n_rows = seq // CMP_STRIDE
    r = _heads_first(kvc, bsz, seq).reshape(bsz, 2 * N_KV_HEADS, n_rows, CMP_STRIDE * HEAD_DIM)
    kvc_c = _compress(r, pe, w1, b1, w2)
    kc = jnp.pad(kvc_c[:, :N_KV_HEADS].astype(BF16), ((0, 0),) * 3 + ((0, LANES - HEAD_DIM),))
    vct = jnp.pad(kvc_c[:, N_KV_HEADS:].transpose(0, 1, 3, 2).astype(BF16),
                  ((0, 0), (0, 0), (0, LANES - HEAD_DIM), (0, 0)))
    a = _nsa_t(qt, kc, vct, selmt, ks_a, vst, kw, vwt, gatet).reshape(bsz * seq, Q_WIDTH)

    return _merge(x2, z, a, sgs, sgn, wbs, wbn, wo)


def kernel(x, norm_mix, w_in, ssm_a_re, ssm_a_im, ssm_log_dt, ssm_b_re, ssm_b_im, ssm_c_re, ssm_c_im,
           ssm_d, ssm_w_glu, cmp_pe_k, cmp_w1_k, cmp_b1_k, cmp_w2_k, cmp_pe_v, cmp_w1_v, cmp_b1_v,
           cmp_w2_v, w_branch_ssm, w_branch_nsa, w_out, norm_ffn, w_ffn_in, ffn_conv_w, ffn_conv_b,
           w_ffn_out, norm_final):
    bsz, seq, _ = x.shape
    depth = norm_mix.shape[0]
    assert seq % ROW_TILE == 0 and seq >= (WINDOW // KEY_TILE + 1) * KEY_TILE
    cos, sin = _rope_tables(seq)
    assert N_SEL <= seq // SEL_BLOCK <= LANES, "selection blocks are mapped onto one vreg of lanes"
    selmt = _sel_fold_matrix(seq // CMP_STRIDE, LANES).T
    half = CMP_STRIDE * HEAD_DIM
    x2 = x.reshape(bsz * seq, D_MODEL)
    for l in range(depth):
        s5p = _prep_s5(ssm_a_re[l], ssm_a_im[l], ssm_log_dt[l], ssm_b_re[l], ssm_b_im[l],
                       ssm_c_re[l], ssm_c_im[l], ssm_d[l], ssm_w_glu[l])
        pe = jnp.stack([cmp_pe_k[l], cmp_pe_v[l]]).reshape(2, 2, half)
        w1 = jnp.stack([cmp_w1_k[l], cmp_w1_v[l]]).astype(BF16)
        b1 = jnp.stack([cmp_b1_k[l], cmp_b1_v[l]]).reshape(2, 1, CMP_HIDDEN)
        w2 = jnp.stack([cmp_w2_k[l], cmp_w2_v[l]]).astype(BF16)
        x2 = _mixer_layer(x2, bsz, seq, cos, sin, norm_mix[l].reshape(1, -1), _prep_w_in(w_in[l]), s5p,
                          pe, w1, b1, w2, w_branch_ssm[l].astype(BF16), w_branch_nsa[l].astype(BF16),
                          w_out[l].astype(BF16), selmt)
        wf = w_ffn_in[l].astype(BF16)
        x2 = _ffn(x2, norm_ffn[l].reshape(1, -1), wf[:, :D_FF], wf[:, D_FF:], ffn_conv_w[l],
                  ffn_conv_b[l].reshape(1, -1), w_ffn_out[l].astype(BF16), norm_final.reshape(1, -1),
                  seq, final=(l == depth - 1))
    return x2.reshape(bsz, seq, D_MODEL)
```

```python
import functools
import math

import numpy as np
import jax
import jax.numpy as jnp
from jax import lax
from jax.experimental import pallas as pl
from jax.experimental.pallas import tpu as pltpu

F32 = jnp.float32
BF16 = jnp.bfloat16

D_MODEL = 1024
SSM_WIDTH = 512
SSM_GROUP = 16
SSM_GROUPS = 32
SSM_STATE = 64
N_Q_HEADS = 8
N_KV_HEADS = 2
HEAD_DIM = 64
GQA_GROUP = 4
CMP_BLOCK = 32
CMP_STRIDE = 16
CMP_HIDDEN = 256
SEL_BLOCK = 64
N_SEL = 16
WINDOW = 512
N_NSA_BRANCH = 3
D_FF = 2816
CONV_WIDTH = 3
RMS_EPS = 1e-6
NEG_INF = -1e30
FORCED = 3
ROPE_THETA = 10000.0
Q_WIDTH = N_Q_HEADS * HEAD_DIM
KV_WIDTH = N_KV_HEADS * HEAD_DIM

LANES = 128
SUBLANES = 8
MXU_DIM = 256
VMEM_LIMIT = 56 * 1024 * 1024

ROW_TILE = 512
S5_TIME_TILE = 64
S5_CHAN_CHUNK = 1024
KEY_TILE = MXU_DIM
Q_TILE = 2 * KEY_TILE
FF_CHUNK = MXU_DIM
FF_SPLIT = 6
Q_SCALE = HEAD_DIM ** -0.5 * math.log2(math.e)
V_ROWS = 80

_C_U = 0
_C_Q = _C_U + SSM_WIDTH
_C_KCMP = _C_Q + Q_WIDTH
_C_VCMP = _C_KCMP + KV_WIDTH
_C_KSEL = _C_VCMP + KV_WIDTH
_C_KWIN = _C_KSEL + KV_WIDTH
_C_VSEL = _C_KWIN + KV_WIDTH
_C_VWIN = _C_VSEL + KV_WIDTH
_C_GATE = _C_VWIN + KV_WIDTH
_C_GSSM = _C_GATE + LANES
_C_GNSA = _C_GSSM + D_MODEL


def _cparams(*sem):
    return pltpu.CompilerParams(dimension_semantics=sem, vmem_limit_bytes=VMEM_LIMIT)


def _gelu(x):
    return x * (0.5 * (1.0 + jnp.tanh(math.sqrt(2.0 / math.pi) * (x + 0.044715 * (x * x * x)))))


def _rms(x, g):
    ms = jnp.mean(x * x, axis=-1, keepdims=True)
    return (x * lax.rsqrt(ms + RMS_EPS)) * g


def _const_spec(shape):
    nd = len(shape)
    return pl.BlockSpec(shape, lambda *_: (0,) * nd)


def _inproj_kernel(x_ref, g_ref, w_ref, cos_ref, sin_ref,
                   u_ref, qt_ref, kvc_ref, ks_ref, kw_ref, vst_ref, vwt_ref, gatet_ref,
                   *, tiles_per_seq):
    tm = x_ref.shape[0]
    tk = KEY_TILE
    n_sub = tm // tk
    tile0 = (pl.program_id(0) % tiles_per_seq) * n_sub
    h = _rms(x_ref[...], g_ref[...]).astype(BF16)
    lane_k = lax.broadcasted_iota(jnp.int32, (tk, LANES), 1)
    row_k = lax.broadcasted_iota(jnp.int32, (tk, LANES), 0)

    def split_heads(a, j):
        piece = a[j * tk:(j + 1) * tk, :]
        return (jnp.where(lane_k < HEAD_DIM, piece, 0.0),
                jnp.where(lane_k < HEAD_DIM, pltpu.roll(piece, HEAD_DIM, 1), 0.0))

    def put_values_t(v, out_ref):
        vt = v.T.astype(BF16)
        tail = jnp.where(lax.broadcasted_iota(jnp.int32, (V_ROWS - HEAD_DIM, tk), 0) == 0, 1.0, 0.0)
        for hh in range(N_KV_HEADS):
            for j in range(n_sub):
                out_ref[0, hh, j, 0:HEAD_DIM, :] = vt[hh * HEAD_DIM:(hh + 1) * HEAD_DIM, j * tk:(j + 1) * tk]
                out_ref[0, hh, j, HEAD_DIM:V_ROWS, :] = tail.astype(BF16)

    def proj(c0, n):
        return jnp.dot(h, w_ref[:, c0:c0 + n], preferred_element_type=F32)

    cos = cos_ref[...]
    sin = sin_ref[...]
    lane = lax.broadcasted_iota(jnp.int32, (tm, LANES), 1)
    first_half = (lane & (HEAD_DIM // 2)) == 0

    def rope(v):
        swapped = jnp.where(first_half, pltpu.roll(v, LANES - HEAD_DIM // 2, 1),
                            pltpu.roll(v, HEAD_DIM // 2, 1))
        return v * cos + swapped * sin

    u_ref[...] = proj(_C_U, SSM_WIDTH)

    tq = Q_TILE
    qt_ref[0, :, :, HEAD_DIM:LANES, :] = jnp.zeros(
        (N_KV_HEADS, tm // tq, LANES - HEAD_DIM, GQA_GROUP * tq), BF16)
    for c in range(Q_WIDTH // LANES):
        if c % 2 == 0:
            q_pair = proj(_C_Q + c * LANES, 2 * LANES)
        q_c = q_pair[:, (c % 2) * LANES:(c % 2 + 1) * LANES]
        qt = (rope(q_c) * Q_SCALE).T.astype(BF16)
        for hl in range(2):
            head = 2 * c + hl
            hkv, g = head // GQA_GROUP, head % GQA_GROUP
            for j in range(tm // tq):
                qt_ref[0, hkv, j, 0:HEAD_DIM, g * tq:(g + 1) * tq] = (
                    qt[hl * HEAD_DIM:(hl + 1) * HEAD_DIM, j * tq:(j + 1) * tq])

    kv_cmp = proj(_C_KCMP, 2 * LANES)
    kvc_ref[:, 0:LANES] = rope(kv_cmp[:, 0:LANES])
    kvc_ref[:, LANES:2 * LANES] = kv_cmp[:, LANES:2 * LANES]

    k_pair = proj(_C_KSEL, 2 * LANES)
    k_sel = rope(k_pair[:, 0:LANES])
    k_win = rope(k_pair[:, LANES:2 * LANES])
    for j in range(n_sub):
        onehot = jnp.where(((tile0 + j) * tk + row_k) >> int(math.log2(SEL_BLOCK)) == lane_k, 1.0, 0.0)
        for hh, (ks_h, kw_h) in enumerate(zip(split_heads(k_sel, j), split_heads(k_win, j))):
            ks_ref[0, hh, j, :, 0:LANES] = ks_h.astype(BF16)
            ks_ref[0, hh, j, :, LANES:2 * LANES] = onehot.astype(BF16)
            kw_ref[0, hh, j] = kw_h.astype(BF16)
    v_pair = proj(_C_VSEL, 2 * LANES)
    put_values_t(v_pair[:, 0:LANES], vst_ref)
    put_values_t(v_pair[:, LANES:2 * LANES], vwt_ref)

    gate_t = jax.nn.sigmoid(proj(_C_GATE, LANES)).T
    for hh in range(N_KV_HEADS):
        gatet_ref[0, hh] = gate_t[hh * HEAD_DIM:hh * HEAD_DIM + 2 * SUBLANES, :]


def _inproj(x2, g, w, cos, sin, seq):
    t = x2.shape[0]
    bsz = t // seq
    tm = ROW_TILE
    spt = seq // tm
    tk = KEY_TILE
    n_sub = tm // tk
    nt = seq // tk
    row = lambda i: (i, 0)
    per_tile = lambda i: (i // spt, 0, i % spt, 0, 0)
    rows_out = lambda n, dt: (jax.ShapeDtypeStruct((t, n), dt), pl.BlockSpec((tm, n), row))
    tile_out = lambda r, c: (jax.ShapeDtypeStruct((bsz, N_KV_HEADS, nt, r, c), BF16),
                             pl.BlockSpec((1, N_KV_HEADS, n_sub, r, c), per_tile))
    outs, out_specs = zip(
        rows_out(SSM_WIDTH, F32),
        (jax.ShapeDtypeStruct((bsz, N_KV_HEADS, seq // Q_TILE, LANES, GQA_GROUP * Q_TILE), BF16),
         pl.BlockSpec((1, N_KV_HEADS, tm // Q_TILE, LANES, GQA_GROUP * Q_TILE), per_tile)),
        rows_out(2 * KV_WIDTH, F32),
        tile_out(tk, 2 * LANES),
        tile_out(tk, LANES),
        tile_out(V_ROWS, tk),
        tile_out(V_ROWS, tk),
        (jax.ShapeDtypeStruct((bsz, N_KV_HEADS, 2 * SUBLANES, seq), F32),
         pl.BlockSpec((1, N_KV_HEADS, 2 * SUBLANES, tm), lambda i: (i // spt, 0, 0, i % spt))),
    )
    return pl.pallas_call(
        functools.partial(_inproj_kernel, tiles_per_seq=spt),
        grid=(t // tm,),
        in_specs=[
            pl.BlockSpec((tm, D_MODEL), row),
            _const_spec((1, D_MODEL)),
            _const_spec((D_MODEL, _C_GSSM)),
            pl.BlockSpec((tm, LANES), lambda i: (i % spt, 0)),
            pl.BlockSpec((tm, LANES), lambda i: (i % spt, 0)),
        ],
        out_specs=list(out_specs),
        out_shape=list(outs),
        compiler_params=_cparams("parallel"),
        name="inproj",
    )(x2, g, w, cos, sin)


def _prep_w_in(w_in):
    o = np.cumsum((0, SSM_WIDTH, Q_WIDTH) + (KV_WIDTH,) * 6 + (N_Q_HEADS * N_NSA_BRANCH, D_MODEL, D_MODEL))
    u, q, k_cmp, v_cmp, k_sel, v_sel, k_win, v_win, gate, g_ssm, g_nsa = (
        w_in[:, o[i]:o[i + 1]] for i in range(11))
    gate = gate.reshape(D_MODEL, N_KV_HEADS, GQA_GROUP, N_NSA_BRANCH).transpose(0, 1, 3, 2)
    gate = gate.reshape(D_MODEL, N_KV_HEADS, N_NSA_BRANCH * GQA_GROUP)
    gate = jnp.pad(gate, ((0, 0), (0, 0), (0, HEAD_DIM - N_NSA_BRANCH * GQA_GROUP))).reshape(D_MODEL, LANES)
    w = jnp.concatenate([u, q, k_cmp, v_cmp, k_sel, k_win, v_sel, v_win, gate, g_ssm, g_nsa], axis=1)
    return w.astype(BF16)


def _rope_tables(seq):
    inv_freq = ROPE_THETA ** (-jnp.arange(0, HEAD_DIM, 2, dtype=F32) / HEAD_DIM)
    ang = jnp.arange(seq, dtype=F32)[:, None] * inv_freq[None, :]
    cos, sin = jnp.cos(ang), jnp.sin(ang)
    return jnp.tile(cos, (1, 4)), jnp.tile(jnp.concatenate([-sin, sin], axis=1), (1, 2))


def _s5_kernel(u_ref, bre_ref, bim_ref, cre_ref, cim_ref, ar_ref, ai_ref, d_ref, wglu_ref,
               z_ref, xr_s, xi_s, str_s, sti_s, io_s):
    nb, ts, _ = u_ref.shape
    rows = ts * nb
    half_in = SSM_WIDTH // 2
    half_st = SSM_GROUPS * SSM_STATE // 2

    @pl.when(pl.program_id(0) == 0)
    def _():
        str_s[...] = jnp.zeros_like(str_s)
        sti_s[...] = jnp.zeros_like(sti_s)

    n_lc = SSM_WIDTH // LANES
    for b in range(nb):
        for lc in range(n_lc):
            io_s[lc, pl.ds(b, ts, stride=nb), :] = u_ref[b, :, lc * LANES:(lc + 1) * LANES]
    u2 = jnp.concatenate([io_s[lc] for lc in range(n_lc)], axis=1)
    ub = u2.astype(BF16)
    for hf in range(2):
        uh = ub[:, hf * half_in:(hf + 1) * half_in]
        xr_s[:, hf * half_st:(hf + 1) * half_st] = jnp.dot(uh, bre_ref[hf], preferred_element_type=F32)
        xi_s[:, hf * half_st:(hf + 1) * half_st] = jnp.dot(uh, bim_ref[hf], preferred_element_type=F32)

    for c in range(2 * half_st // S5_CHAN_CHUNK):
        sl = slice(c * S5_CHAN_CHUNK, (c + 1) * S5_CHAN_CHUNK)
        ar = jnp.broadcast_to(ar_ref[:, sl], (nb, S5_CHAN_CHUNK))
        ai = jnp.broadcast_to(ai_ref[:, sl], (nb, S5_CHAN_CHUNK))

        def step(t, carry, sl=sl, ar=ar, ai=ai):
            xr, xi = carry
            r0 = pl.multiple_of(t * nb, nb)
            nr = ar * xr - ai * xi + xr_s[pl.ds(r0, nb), sl]
            ni = ar * xi + ai * xr + xi_s[pl.ds(r0, nb), sl]
            xr_s[pl.ds(r0, nb), sl] = nr
            xi_s[pl.ds(r0, nb), sl] = ni
            return nr, ni

        xr, xi = lax.fori_loop(0, ts, step, (str_s[:, sl], sti_s[:, sl]), unroll=8)
        str_s[:, sl] = xr
        sti_s[:, sl] = xi

    ys = []
    for hf in range(2):
        st = slice(hf * half_st, (hf + 1) * half_st)
        ys.append(jnp.dot(xr_s[:, st].astype(BF16), cre_ref[hf], preferred_element_type=F32)
                  - jnp.dot(xi_s[:, st].astype(BF16), cim_ref[hf], preferred_element_type=F32))
    y = jnp.concatenate(ys, axis=1) + d_ref[...] * u2
    zz = jnp.dot(_gelu(y).astype(BF16), wglu_ref[...], preferred_element_type=F32)
    z = zz[:, :SSM_WIDTH] * jax.nn.sigmoid(zz[:, SSM_WIDTH:])
    for lc in range(n_lc):
        io_s[lc] = z[:, lc * LANES:(lc + 1) * LANES]
    for b in range(nb):
        for lc in range(n_lc):
            z_ref[b, :, lc * LANES:(lc + 1) * LANES] = io_s[lc, pl.ds(b, ts, stride=nb), :]


def _s5(u, p):
    nb, seq, _ = u.shape
    assert nb == SUBLANES, "the scan maps the batch onto the 8 sublanes"
    ts = S5_TIME_TILE
    nst = SSM_GROUPS * SSM_STATE
    return pl.pallas_call(
        _s5_kernel,
        grid=(seq // ts,),
        in_specs=[
            pl.BlockSpec((nb, ts, SSM_WIDTH), lambda i: (0, i, 0)),
            _const_spec(p["bre"].shape), _const_spec(p["bim"].shape),
            _const_spec(p["cre"].shape), _const_spec(p["cim"].shape),
            _const_spec((1, nst)), _const_spec((1, nst)),
            _const_spec((1, SSM_WIDTH)), _const_spec((SSM_WIDTH, 2 * SSM_WIDTH)),
        ],
        out_specs=pl.BlockSpec((nb, ts, SSM_WIDTH), lambda i: (0, i, 0)),
        out_shape=jax.ShapeDtypeStruct((nb, seq, SSM_WIDTH), F32),
        scratch_shapes=[
            pltpu.VMEM((ts * nb, nst), F32), pltpu.VMEM((ts * nb, nst), F32),
            pltpu.VMEM((nb, nst), F32), pltpu.VMEM((nb, nst), F32),
            pltpu.VMEM((SSM_WIDTH // LANES, ts * nb, LANES), F32),
        ],
        compiler_params=_cparams("arbitrary"),
        name="s5",
    )(u, p["bre"], p["bim"], p["cre"], p["cim"], p["ar"], p["ai"], p["d"], p["wglu"])


def _prep_s5(a_re, a_im, log_dt, b_re, b_im, c_re, c_im, d, w_glu):
    dt = jnp.exp(log_dt)[:, None]
    mag = jnp.exp(a_re * dt)
    ab_re = mag * jnp.cos(a_im * dt)
    ab_im = mag * jnp.sin(a_im * dt)
    den = a_re * a_re + a_im * a_im
    f_re = ((ab_re - 1.0) * a_re + ab_im * a_im) / den
    f_im = (ab_im * a_re - (ab_re - 1.0) * a_im) / den
    fb_re = f_re[:, :, None] * b_re - f_im[:, :, None] * b_im
    fb_im = f_re[:, :, None] * b_im + f_im[:, :, None] * b_re
    gh = SSM_GROUPS // 2
    eye = jnp.eye(gh, dtype=F32)

    def blk_in(w):
        w = w.reshape(2, gh, SSM_STATE, SSM_GROUP)
        return jnp.einsum("kgph,gj->kghjp", w, eye).reshape(2, gh * SSM_GROUP, gh * SSM_STATE).astype(BF16)

    def blk_out(w):
        w = w.reshape(2, gh, SSM_GROUP, SSM_STATE)
        return jnp.einsum("kghp,gj->kgpjh", w, eye).reshape(2, gh * SSM_STATE, gh * SSM_GROUP).astype(BF16)

    return dict(bre=blk_in(fb_re), bim=blk_in(fb_im), cre=blk_out(c_re), cim=blk_out(c_im),
                ar=ab_re.reshape(1, -1), ai=ab_im.reshape(1, -1), d=d.reshape(1, -1),
                wglu=w_glu.astype(BF16))


def _compress_kernel(r_ref, pe_ref, w1_ref, b1_ref, w2_ref, o_ref):
    r = r_ref[0, 0]
    n = r.shape[0]
    half = CMP_STRIDE * HEAD_DIM
    h1 = jnp.dot((r + pe_ref[0, 0:1, :]).astype(BF16), w1_ref[0, :half, :], preferred_element_type=F32)
    h2 = jnp.dot((r + pe_ref[0, 1:2, :]).astype(BF16), w1_ref[0, half:, :], preferred_element_type=F32)
    hid = h1 + pltpu.roll(h2, n - 1, 0) + b1_ref[0]
    out = jnp.dot(_gelu(hid).astype(BF16), w2_ref[0], preferred_element_type=F32)
    rowi = lax.broadcasted_iota(jnp.int32, out.shape, 0)
    o_ref[0, 0] = jnp.where(rowi < n - 1, out, 0.0)


def _compress(r, pe, w1, b1, w2):
    bsz, four, n, width = r.shape
    return pl.pallas_call(
        _compress_kernel,
        grid=(bsz, four),
        in_specs=[
            pl.BlockSpec((1, 1, n, width), lambda b, c: (b, c, 0, 0)),
            pl.BlockSpec((1, 2, width), lambda b, c: (c // N_KV_HEADS, 0, 0)),
            pl.BlockSpec((1, 2 * width, CMP_HIDDEN), lambda b, c: (c // N_KV_HEADS, 0, 0)),
            pl.BlockSpec((1, 1, CMP_HIDDEN), lambda b, c: (c // N_KV_HEADS, 0, 0)),
            pl.BlockSpec((1, CMP_HIDDEN, HEAD_DIM), lambda b, c: (c // N_KV_HEADS, 0, 0)),
        ],
        out_specs=pl.BlockSpec((1, 1, n, HEAD_DIM), lambda b, c: (b, c, 0, 0)),
        out_shape=jax.ShapeDtypeStruct((bsz, four, n, HEAD_DIM), F32),
        compiler_params=_cparams("parallel", "parallel"),
        name="compress",
    )(r, pe, w1, b1, w2)


def _sel_fold_matrix(n_cmp_pad, n_blk):
    ratio = SEL_BLOCK // CMP_STRIDE
    span = ratio + CMP_BLOCK // CMP_STRIDE - 1
    pad_l = CMP_BLOCK // CMP_STRIDE - 1
    i = np.arange(n_cmp_pad)[:, None]
    j = np.arange(n_blk)[None, :]
    lo = ratio * j - pad_l
    return jnp.asarray(((i >= lo) & (i < lo + span)), dtype=BF16)


def _heads_first(a, bsz, seq):
    return a.reshape(bsz, seq, -1, HEAD_DIM).transpose(0, 2, 1, 3)


def _softmax_pv_t(st, vt_tile, m_ref, acc_ref, cs, mask):
    if mask is not None:
        st = jnp.where(mask, st, NEG_INF)
    m_old = m_ref[:, cs]
    m_new = jnp.maximum(m_old, jnp.max(st, axis=0, keepdims=True))
    pt = jnp.exp2(st - m_new).astype(BF16)
    acc_ref[:, cs] = jnp.exp2(m_old - m_new) * acc_ref[:, cs] + jnp.dot(
        vt_tile, pt, preferred_element_type=F32)
    m_ref[:, cs] = m_new


def _nsa_t_kernel(qt_ref, kc_ref, vct_ref, selmt_ref, ks_ref, vst_ref, kw_ref, vwt_ref, gatet_ref,
                  o_ref, lhs_s, m_s, acc_s, mw_s, accw_s, ocmp_s, pg_s, scmp_s, stw_s, sta_s, stb_s):
    cols = qt_ref.shape[4]
    tq = cols // GQA_GROUP
    n_cmp = kc_ref.shape[2]
    n_blk = selmt_ref.shape[0]
    kpq = tq // KEY_TILE
    assert kpq == 2, "the selected-branch loop walks the key tiles in pairs"
    n_wt = WINDOW // KEY_TILE + kpq
    qi = pl.program_id(2)
    s0 = qi * tq
    wt0 = jnp.maximum(kpq * qi - WINDOW // KEY_TILE, 0)
    heads = [slice(g * tq, (g + 1) * tq) for g in range(GQA_GROUP)]

    lhs_s[0:LANES, :] = qt_ref[0, 0, 0]
    t_row = s0 + lax.broadcasted_iota(jnp.int32, (1, tq), 1)

    scmp_s[...] = jnp.dot(kc_ref[0, 0], lhs_s[0:LANES, :], preferred_element_type=F32)
    for i in range(n_wt):
        stw_s[i] = jnp.dot(kw_ref[0, 0, wt0 + i], lhs_s[0:LANES, :], preferred_element_type=F32)

    cmp_end = lax.broadcasted_iota(jnp.int32, (n_cmp, 1), 0) * CMP_STRIDE + (CMP_BLOCK - 1)
    cmask = cmp_end <= t_row
    for g, cs in enumerate(heads):
        s = jnp.where(cmask, scmp_s[:, cs], NEG_INF)
        e = jnp.exp2(s - jnp.max(s, axis=0, keepdims=True))
        p = jnp.where(cmask, e, 0.0) / jnp.sum(e, axis=0, keepdims=True)
        ocmp_s[:, cs] = jnp.dot(vct_ref[0, 0], p.astype(BF16), preferred_element_type=F32)
        if g == 0:
            pg_s[...] = p
        else:
            pg_s[...] += p

    pg = pg_s[...]
    selmt = selmt_ref[...]
    hi = pg.astype(BF16)
    r1 = pg - hi.astype(F32)
    mid = r1.astype(BF16)
    lo = (r1 - mid.astype(F32)).astype(BF16)
    bs = (jnp.dot(selmt, hi, preferred_element_type=F32) + jnp.dot(selmt, mid, preferred_element_type=F32)
          + jnp.dot(selmt, lo, preferred_element_type=F32))
    j_idx = lax.broadcasted_iota(jnp.int32, (n_blk, 1), 0)
    cur = t_row >> int(math.log2(SEL_BLOCK))
    forced = (j_idx == 0) | (j_idx == cur) | (j_idx == cur - 1)
    causal_b = j_idx * SEL_BLOCK <= t_row
    work = jnp.where(causal_b & ~forced, bs, NEG_INF)
    idx = lax.broadcasted_iota(jnp.int32, (n_blk, tq), 0).astype(F32)
    picked = jnp.where(forced, 1.0, 0.0)
    for _ in range(min(N_SEL, n_blk) - FORCED):
        mx = jnp.max(work, axis=0, keepdims=True)
        first = jnp.min(jnp.where(work == mx, idx, float(n_blk)), axis=0, keepdims=True)
        hit = idx == first
        picked = jnp.where(hit, 1.0, picked)
        work = jnp.where(hit, -jnp.inf, work)
    bias = jnp.where((picked > 0.0) & causal_b, 0.0, NEG_INF).astype(BF16)
    for cs in heads:
        lhs_s[LANES:2 * LANES, cs] = bias

    m_s[...] = jnp.full(m_s.shape, NEG_INF, F32)
    acc_s[...] = jnp.zeros(acc_s.shape, F32)
    mw_s[...] = jnp.full(mw_s.shape, NEG_INF, F32)
    accw_s[...] = jnp.zeros(accw_s.shape, F32)
    key_sub = lax.broadcasted_iota(jnp.int32, (KEY_TILE, 1), 0)

    for i in range(n_wt):
        pos = (wt0 + i) * KEY_TILE + key_sub
        mask = (pos <= t_row) & (pos > t_row - WINDOW)
        for cs in heads:
            _softmax_pv_t(stw_s[i, :, cs], vwt_ref[0, 0, wt0 + i], mw_s, accw_s, cs, mask)

    def produce(st_ref, kt):
        for cs in heads:
            st_ref[:, cs] = jnp.dot(ks_ref[0, 0, kt], lhs_s[:, cs], preferred_element_type=F32)

    def consume(st_ref, kt, diag):
        mask = (kt * KEY_TILE + key_sub <= t_row) if diag else None
        for cs in heads:
            _softmax_pv_t(st_ref[:, cs], vst_ref[0, 0, kt], m_s, acc_s, cs, mask)

    produce(sta_s, 0)

    def sel_pair(j, carry):
        kt = 2 * j
        produce(stb_s, kt + 1)
        consume(sta_s, kt, False)
        produce(sta_s, kt + 2)
        consume(stb_s, kt + 1, False)
        return carry

    lax.fori_loop(0, qi, sel_pair, 0)
    produce(stb_s, kpq * qi + 1)
    consume(sta_s, kpq * qi, True)
    consume(stb_s, kpq * qi + 1, True)

    gt = gatet_ref[0, 0]
    outs = []
    for g, cs in enumerate(heads):
        a_sel = acc_s[:, cs]
        a_win = accw_s[:, cs]
        o_t = (gt[g:g + 1, :] * ocmp_s[0:HEAD_DIM, cs]
               + gt[GQA_GROUP + g:GQA_GROUP + g + 1, :]
               * (a_sel[0:HEAD_DIM, :] / a_sel[HEAD_DIM:HEAD_DIM + 1, :])
               + gt[2 * GQA_GROUP + g:2 * GQA_GROUP + g + 1, :]
               * (a_win[0:HEAD_DIM, :] / a_win[HEAD_DIM:HEAD_DIM + 1, :]))
        outs.append(o_t.T)
    o_ref[0] = jnp.concatenate(outs, axis=1)


def _nsa_t(qt, kc, vct, selmt, ks_a, vst, kw, vwt, gatet):
    bsz, _, nq, _, cols = qt.shape
    tq = cols // GQA_GROUP
    nt = ks_a.shape[2]
    n_cmp = kc.shape[2]
    bh4 = lambda b, h, i: (b, h, 0, 0)
    bh5 = lambda b, h, i: (b, h, 0, 0, 0)
    return pl.pallas_call(
        _nsa_t_kernel,
        grid=(bsz, N_KV_HEADS, nq),
        in_specs=[
            pl.BlockSpec((1, 1, 1, LANES, cols), lambda b, h, i: (b, h, i, 0, 0)),
            pl.BlockSpec((1, 1, n_cmp, LANES), bh4),
            pl.BlockSpec((1, 1, LANES, n_cmp), bh4),
            _const_spec(selmt.shape),
            pl.BlockSpec((1, 1, nt, KEY_TILE, 2 * LANES), bh5),
            pl.BlockSpec((1, 1, nt, V_ROWS, KEY_TILE), bh5),
            pl.BlockSpec((1, 1, nt, KEY_TILE, LANES), bh5),
            pl.BlockSpec((1, 1, nt, V_ROWS, KEY_TILE), bh5),
            pl.BlockSpec((1, 1, 2 * SUBLANES, tq), lambda b, h, i: (b, h, 0, i)),
        ],
        out_specs=pl.BlockSpec((1, tq, GQA_GROUP * HEAD_DIM), lambda b, h, i: (b, i, h)),
        out_shape=jax.ShapeDtypeStruct((bsz, nq * tq, Q_WIDTH), F32),
        scratch_shapes=[
            pltpu.VMEM((2 * LANES, cols), BF16),
            pltpu.VMEM((1, cols), F32), pltpu.VMEM((V_ROWS, cols), F32),
            pltpu.VMEM((1, cols), F32), pltpu.VMEM((V_ROWS, cols), F32),
            pltpu.VMEM((LANES, cols), F32),
            pltpu.VMEM((n_cmp, tq), F32),
            pltpu.VMEM((n_cmp, cols), F32),
            pltpu.VMEM((WINDOW // KEY_TILE + tq // KEY_TILE, KEY_TILE, cols), F32),
            pltpu.VMEM((KEY_TILE, cols), F32), pltpu.VMEM((KEY_TILE, cols), F32),
        ],
        compiler_params=_cparams("parallel", "parallel", "arbitrary"),
        name="nsa",
    )(qt, kc, vct, selmt, ks_a, vst, kw, vwt, gatet)


def _merge_kernel(x_ref, z_ref, a_ref, g_ref, wg_ref, wbs_ref, wbn_ref, wo_ref, o_ref, merged_s):
    x = x_ref[...]
    h = _rms(x, g_ref[...]).astype(BF16)
    z = z_ref[...].astype(BF16)
    a = a_ref[...].astype(BF16)
    for c in range(D_MODEL // MXU_DIM):
        cs = slice(c * MXU_DIM, (c + 1) * MXU_DIM)
        gs = slice(D_MODEL + c * MXU_DIM, D_MODEL + (c + 1) * MXU_DIM)
        sg_ssm = jax.nn.sigmoid(jnp.dot(h, wg_ref[:, cs], preferred_element_type=F32))
        sg_nsa = jax.nn.sigmoid(jnp.dot(h, wg_ref[:, gs], preferred_element_type=F32))
        merged_s[:, cs] = (sg_ssm * jnp.dot(z, wbs_ref[:, cs], preferred_element_type=F32)
                           + sg_nsa * jnp.dot(a, wbn_ref[:, cs], preferred_element_type=F32)).astype(BF16)
    o_ref[...] = x + jnp.dot(merged_s[...], wo_ref[...], preferred_element_type=F32)


def _merge(x2, z, a, g, wg, wbs, wbn, wo):
    t = x2.shape[0]
    tm = ROW_TILE
    row = lambda i: (i, 0)
    return pl.pallas_call(
        _merge_kernel,
        grid=(t // tm,),
        in_specs=[
            pl.BlockSpec((tm, D_MODEL), row), pl.BlockSpec((tm, SSM_WIDTH), row),
            pl.BlockSpec((tm, Q_WIDTH), row),
            _const_spec((1, D_MODEL)), _const_spec(wg.shape),
            _const_spec(wbs.shape), _const_spec(wbn.shape), _const_spec(wo.shape),
        ],
        out_specs=pl.BlockSpec((tm, D_MODEL), row),
        out_shape=jax.ShapeDtypeStruct((t, D_MODEL), F32),
        scratch_shapes=[pltpu.VMEM((tm, D_MODEL), BF16)],
        compiler_params=_cparams("parallel"),
        name="merge",
    )(x2, z, a, g, wg, wbs, wbn, wo)


def _ffn_kernel(x_ref, xh_ref, g_ref, wa_ref, wb_ref, cw_ref, cb_ref, wo_ref, gf_ref, o_ref, a_s, act_s,
                *, tiles_per_seq, final):
    tm = x_ref.shape[0]
    halo = xh_ref.shape[0]
    first = (pl.program_id(0) % tiles_per_seq) == 0
    x = x_ref[...]
    g = g_ref[...]
    h = _rms(x, g).astype(BF16)
    hh = _rms(xh_ref[...], g).astype(BF16)
    keep_halo = jnp.where(first, 0.0, 1.0)
    n_chunks = D_FF // FF_CHUNK
    y = x
    for c in range(n_chunks):
        cs = slice(c * FF_CHUNK, (c + 1) * FF_CHUNK)
        a_buf = a_s.at[c % 2]
        a_buf[0:halo, :] = jnp.dot(hh, wa_ref[:, cs], preferred_element_type=F32) * keep_halo
        a_buf[halo:halo + tm, :] = jnp.dot(h, wa_ref[:, cs], preferred_element_type=F32)
        b = jnp.dot(h, wb_ref[:, cs], preferred_element_type=F32)
        cw = cw_ref[:, cs]
        conv = cb_ref[:, cs]
        for k in range(CONV_WIDTH):
            off = halo - (CONV_WIDTH - 1) + k
            conv = conv + cw[k:k + 1, :] * a_buf[off:off + tm, :]
        act_s[:, cs] = (_gelu(conv) * b).astype(BF16)
        if c + 1 in (FF_SPLIT, n_chunks):
            ks = slice((0 if c + 1 == FF_SPLIT else FF_SPLIT) * FF_CHUNK, (c + 1) * FF_CHUNK)
            y = y + jnp.dot(act_s[:, ks], wo_ref[ks, :], preferred_element_type=F32)
    if final:
        y = _rms(y, gf_ref[...])
    o_ref[...] = y


def _ffn(x2, g, wa, wb, cw, cb, wo, gf, seq, final):
    t = x2.shape[0]
    tm = ROW_TILE
    halo = SUBLANES
    tps = seq // tm
    row = lambda i: (i, 0)
    return pl.pallas_call(
        functools.partial(_ffn_kernel, tiles_per_seq=tps, final=final),
        grid=(t // tm,),
        in_specs=[
            pl.BlockSpec((tm, D_MODEL), row),
            pl.BlockSpec((halo, D_MODEL), lambda i: (jnp.maximum(i * (tm // halo) - 1, 0), 0)),
            _const_spec((1, D_MODEL)),
            _const_spec(wa.shape), _const_spec(wb.shape),
            _const_spec(cw.shape), _const_spec(cb.shape), _const_spec(wo.shape),
            _const_spec((1, D_MODEL)),
        ],
        out_specs=pl.BlockSpec((tm, D_MODEL), row),
        out_shape=jax.ShapeDtypeStruct((t, D_MODEL), F32),
        scratch_shapes=[pltpu.VMEM((2, tm + halo, FF_CHUNK), F32), pltpu.VMEM((tm, D_FF), BF16)],
        compiler_params=_cparams("parallel"),
        name="ffn",
    )(x2, x2, g, wa, wb, cw, cb, wo, gf)


def _mixer_layer(x2, bsz, seq, cos, sin, norm_g, w_in, s5p, pe, w1, b1, w2, wbs, wbn, wo, selmt):
    u, qt, kvc, ks_a, kw, vst, vwt, gatet = _inproj(x2, norm_g, w_in[:, :_C_GSSM], cos, sin, seq)

    z = _s5(u.reshape(bsz, seq, SSM_WIDTH), s5p).reshape(bsz * seq, SSM_WIDTH)

    n_rows = seq // CMP_STRIDE
    r = _heads_first(kvc, bsz, seq).reshape(bsz, 2 * N_KV_HEADS, n_rows, CMP_STRIDE * HEAD_DIM)
    kvc_c = _compress(r, pe, w1, b1, w2)
    kc = jnp.pad(kvc_c[:, :N_KV_HEADS].astype(BF16), ((0, 0),) * 3 + ((0, LANES - HEAD_DIM),))
    vct = jnp.pad(kvc_c[:, N_KV_HEADS:].transpose(0, 1, 3, 2).astype(BF16),
                  ((0, 0), (0, 0), (0, LANES - HEAD_DIM), (0, 0)))
    a = _nsa_t(qt, kc, vct, selmt, ks_a, vst, kw, vwt, gatet).reshape(bsz * seq, Q_WIDTH)

    return _merge(x2, z, a, norm_g, w_in[:, _C_GSSM:], wbs, wbn, wo)


def kernel(x, norm_mix, w_in, ssm_a_re, ssm_a_im, ssm_log_dt, ssm_b_re, ssm_b_im, ssm_c_re, ssm_c_im,
           ssm_d, ssm_w_glu, cmp_pe_k, cmp_w1_k, cmp_b1_k, cmp_w2_k, cmp_pe_v, cmp_w1_v, cmp_b1_v,
           cmp_w2_v, w_branch_ssm, w_branch_nsa, w_out, norm_ffn, w_ffn_in, ffn_conv_w, ffn_conv_b,
           w_ffn_out, norm_final):
    bsz, seq, _ = x.shape
    depth = norm_mix.shape[0]
    assert seq % ROW_TILE == 0 and ROW_TILE % Q_TILE == 0 and seq >= WINDOW + Q_TILE
    cos, sin = _rope_tables(seq)
    assert N_SEL <= seq // SEL_BLOCK <= LANES, "selection blocks are mapped onto one vreg of lanes"
    selmt = _sel_fold_matrix(seq // CMP_STRIDE, LANES).T
    half = CMP_STRIDE * HEAD_DIM
    x2 = x.reshape(bsz * seq, D_MODEL)
    for l in range(depth):
        s5p = _prep_s5(ssm_a_re[l], ssm_a_im[l], ssm_log_dt[l], ssm_b_re[l], ssm_b_im[l],
                       ssm_c_re[l], ssm_c_im[l], ssm_d[l], ssm_w_glu[l])
        pe = jnp.stack([cmp_pe_k[l], cmp_pe_v[l]]).reshape(2, 2, half)
        w1 = jnp.stack([cmp_w1_k[l], cmp_w1_v[l]]).astype(BF16)
        b1 = jnp.stack([cmp_b1_k[l], cmp_b1_v[l]]).reshape(2, 1, CMP_HIDDEN)
        w2 = jnp.stack([cmp_w2_k[l], cmp_w2_v[l]]).astype(BF16)
        x2 = _mixer_layer(x2, bsz, seq, cos, sin, norm_mix[l].reshape(1, -1), _prep_w_in(w_in[l]), s5p,
                          pe, w1, b1, w2, w_branch_ssm[l].astype(BF16), w_branch_nsa[l].astype(BF16),
                          w_out[l].astype(BF16), selmt)
        wf = w_ffn_in[l].astype(BF16)
        x2 = _ffn(x2, norm_ffn[l].reshape(1, -1), wf[:, :D_FF], wf[:, D_FF:], ffn_conv_w[l],
                  ffn_conv_b[l].reshape(1, -1), w_ffn_out[l].astype(BF16), norm_final.reshape(1, -1),
                  seq, final=(l == depth - 1))
    return x2.reshape(bsz, seq, D_MODEL)
```

```python
import functools
import math

import numpy as np
import jax
import jax.numpy as jnp
from jax import lax
from jax.experimental import pallas as pl
from jax.experimental.pallas import tpu as pltpu

F32 = jnp.float32
BF16 = jnp.bfloat16

D_MODEL = 1024
SSM_WIDTH = 512
SSM_GROUP = 16
SSM_GROUPS = 32
SSM_STATE = 64
N_Q_HEADS = 8
N_KV_HEADS = 2
HEAD_DIM = 64
GQA_GROUP = 4
CMP_BLOCK = 32
CMP_STRIDE = 16
CMP_HIDDEN = 256
SEL_BLOCK = 64
N_SEL = 16
WINDOW = 512
N_NSA_BRANCH = 3
D_FF = 2816
CONV_WIDTH = 3
RMS_EPS = 1e-6
NEG_INF = -1e30
FORCED = 3
ROPE_THETA = 10000.0
Q_WIDTH = N_Q_HEADS * HEAD_DIM
KV_WIDTH = N_KV_HEADS * HEAD_DIM

LANES = 128
SUBLANES = 8
MXU_DIM = 256
VMEM_LIMIT = 56 * 1024 * 1024

ROW_TILE = 512
S5_TIME_TILE = 64
S5_CHAN_CHUNK = 1024
KEY_TILE = MXU_DIM
Q_TILE = 2 * KEY_TILE
FF_CHUNK = MXU_DIM
FF_GROUP = D_FF // FF_CHUNK
Q_SCALE = HEAD_DIM ** -0.5 * math.log2(math.e)
V_ROWS = 80

_C_U = 0
_C_Q = _C_U + SSM_WIDTH
_C_KCMP = _C_Q + Q_WIDTH
_C_VCMP = _C_KCMP + KV_WIDTH
_C_KSEL = _C_VCMP + KV_WIDTH
_C_KWIN = _C_KSEL + KV_WIDTH
_C_VSEL = _C_KWIN + KV_WIDTH
_C_VWIN = _C_VSEL + KV_WIDTH
_C_GATE = _C_VWIN + KV_WIDTH
_C_GSSM = _C_GATE + LANES
_C_GNSA = _C_GSSM + D_MODEL


def _cparams(*sem):
    return pltpu.CompilerParams(dimension_semantics=sem, vmem_limit_bytes=VMEM_LIMIT)


def _gelu(x):
    return x * (0.5 * (1.0 + jnp.tanh(math.sqrt(2.0 / math.pi) * (x + 0.044715 * (x * x * x)))))


def _rms(x, g):
    ms = jnp.mean(x * x, axis=-1, keepdims=True)
    return (x * lax.rsqrt(ms + RMS_EPS)) * g


def _const_spec(shape):
    nd = len(shape)
    return pl.BlockSpec(shape, lambda *_: (0,) * nd)


def _inproj_kernel(x_ref, g_ref, w_ref, cos_ref, sin_ref,
                   u_ref, qt_ref, kvc_ref, ks_ref, kw_ref, vst_ref, vwt_ref, gatet_ref,
                   *, tiles_per_seq):
    tm = x_ref.shape[0]
    tk = KEY_TILE
    n_sub = tm // tk
    tile0 = (pl.program_id(0) % tiles_per_seq) * n_sub
    h = _rms(x_ref[...], g_ref[...]).astype(BF16)
    lane_k = lax.broadcasted_iota(jnp.int32, (tk, LANES), 1)
    row_k = lax.broadcasted_iota(jnp.int32, (tk, LANES), 0)

    def split_heads(a, j):
        piece = a[j * tk:(j + 1) * tk, :]
        return (jnp.where(lane_k < HEAD_DIM, piece, 0.0),
                jnp.where(lane_k < HEAD_DIM, pltpu.roll(piece, HEAD_DIM, 1), 0.0))

    def put_values_t(v, out_ref):
        vt = v.T.astype(BF16)
        tail = jnp.where(lax.broadcasted_iota(jnp.int32, (V_ROWS - HEAD_DIM, tk), 0) == 0, 1.0, 0.0)
        for hh in range(N_KV_HEADS):
            for j in range(n_sub):
                out_ref[0, hh, j, 0:HEAD_DIM, :] = vt[hh * HEAD_DIM:(hh + 1) * HEAD_DIM, j * tk:(j + 1) * tk]
                out_ref[0, hh, j, HEAD_DIM:V_ROWS, :] = tail.astype(BF16)

    def proj(c0, n):
        return jnp.dot(h, w_ref[:, c0:c0 + n], preferred_element_type=F32)

    cos = cos_ref[...]
    sin = sin_ref[...]
    lane = lax.broadcasted_iota(jnp.int32, (tm, LANES), 1)
    first_half = (lane & (HEAD_DIM // 2)) == 0

    def rope(v):
        swapped = jnp.where(first_half, pltpu.roll(v, LANES - HEAD_DIM // 2, 1),
                            pltpu.roll(v, HEAD_DIM // 2, 1))
        return v * cos + swapped * sin

    u_ref[...] = proj(_C_U, SSM_WIDTH)

    tq = Q_TILE
    qt_ref[0, :, :, HEAD_DIM:LANES, :] = jnp.zeros(
        (N_KV_HEADS, tm // tq, LANES - HEAD_DIM, GQA_GROUP * tq), BF16)
    for c in range(Q_WIDTH // LANES):
        if c % 2 == 0:
            q_pair = proj(_C_Q + c * LANES, 2 * LANES)
        q_c = q_pair[:, (c % 2) * LANES:(c % 2 + 1) * LANES]
        qt = (rope(q_c) * Q_SCALE).T.astype(BF16)
        for hl in range(2):
            head = 2 * c + hl
            hkv, g = head // GQA_GROUP, head % GQA_GROUP
            for j in range(tm // tq):
                qt_ref[0, hkv, j, 0:HEAD_DIM, g * tq:(g + 1) * tq] = (
                    qt[hl * HEAD_DIM:(hl + 1) * HEAD_DIM, j * tq:(j + 1) * tq])

    kv_cmp = proj(_C_KCMP, 2 * LANES)
    kvc_ref[:, 0:LANES] = rope(kv_cmp[:, 0:LANES])
    kvc_ref[:, LANES:2 * LANES] = kv_cmp[:, LANES:2 * LANES]

    k_pair = proj(_C_KSEL, 2 * LANES)
    k_sel = rope(k_pair[:, 0:LANES])
    k_win = rope(k_pair[:, LANES:2 * LANES])
    for j in range(n_sub):
        onehot = jnp.where(((tile0 + j) * tk + row_k) >> int(math.log2(SEL_BLOCK)) == lane_k, 1.0, 0.0)
        for hh, (ks_h, kw_h) in enumerate(zip(split_heads(k_sel, j), split_heads(k_win, j))):
            ks_ref[0, hh, j, :, 0:LANES] = ks_h.astype(BF16)
            ks_ref[0, hh, j, :, LANES:2 * LANES] = onehot.astype(BF16)
            kw_ref[0, hh, j] = kw_h.astype(BF16)
    v_pair = proj(_C_VSEL, 2 * LANES)
    put_values_t(v_pair[:, 0:LANES], vst_ref)
    put_values_t(v_pair[:, LANES:2 * LANES], vwt_ref)

    gate_t = jax.nn.sigmoid(proj(_C_GATE, LANES)).T
    for hh in range(N_KV_HEADS):
        gatet_ref[0, hh] = gate_t[hh * HEAD_DIM:hh * HEAD_DIM + 2 * SUBLANES, :]


def _inproj(x2, g, w, cos, sin, seq):
    t = x2.shape[0]
    bsz = t // seq
    tm = ROW_TILE
    spt = seq // tm
    tk = KEY_TILE
    n_sub = tm // tk
    nt = seq // tk
    row = lambda i: (i, 0)
    per_tile = lambda i: (i // spt, 0, i % spt, 0, 0)
    rows_out = lambda n, dt: (jax.ShapeDtypeStruct((t, n), dt), pl.BlockSpec((tm, n), row))
    tile_out = lambda r, c: (jax.ShapeDtypeStruct((bsz, N_KV_HEADS, nt, r, c), BF16),
                             pl.BlockSpec((1, N_KV_HEADS, n_sub, r, c), per_tile))
    outs, out_specs = zip(
        rows_out(SSM_WIDTH, F32),
        (jax.ShapeDtypeStruct((bsz, N_KV_HEADS, seq // Q_TILE, LANES, GQA_GROUP * Q_TILE), BF16),
         pl.BlockSpec((1, N_KV_HEADS, tm // Q_TILE, LANES, GQA_GROUP * Q_TILE), per_tile)),
        rows_out(2 * KV_WIDTH, F32),
        tile_out(tk, 2 * LANES),
        tile_out(tk, LANES),
        tile_out(V_ROWS, tk),
        tile_out(V_ROWS, tk),
        (jax.ShapeDtypeStruct((bsz, N_KV_HEADS, 2 * SUBLANES, seq), F32),
         pl.BlockSpec((1, N_KV_HEADS, 2 * SUBLANES, tm), lambda i: (i // spt, 0, 0, i % spt))),
    )
    return pl.pallas_call(
        functools.partial(_inproj_kernel, tiles_per_seq=spt),
        grid=(t // tm,),
        in_specs=[
            pl.BlockSpec((tm, D_MODEL), row),
            _const_spec((1, D_MODEL)),
            _const_spec((D_MODEL, _C_GSSM)),
            pl.BlockSpec((tm, LANES), lambda i: (i % spt, 0)),
            pl.BlockSpec((tm, LANES), lambda i: (i % spt, 0)),
        ],
        out_specs=list(out_specs),
        out_shape=list(outs),
        compiler_params=_cparams("parallel"),
        name="inproj",
    )(x2, g, w, cos, sin)


def _prep_w_in(w_in):
    o = np.cumsum((0, SSM_WIDTH, Q_WIDTH) + (KV_WIDTH,) * 6 + (N_Q_HEADS * N_NSA_BRANCH, D_MODEL, D_MODEL))
    u, q, k_cmp, v_cmp, k_sel, v_sel, k_win, v_win, gate, g_ssm, g_nsa = (
        w_in[:, o[i]:o[i + 1]] for i in range(11))
    gate = gate.reshape(D_MODEL, N_KV_HEADS, GQA_GROUP, N_NSA_BRANCH).transpose(0, 1, 3, 2)
    gate = gate.reshape(D_MODEL, N_KV_HEADS, N_NSA_BRANCH * GQA_GROUP)
    gate = jnp.pad(gate, ((0, 0), (0, 0), (0, HEAD_DIM - N_NSA_BRANCH * GQA_GROUP))).reshape(D_MODEL, LANES)
    w = jnp.concatenate([u, q, k_cmp, v_cmp, k_sel, k_win, v_sel, v_win, gate, g_ssm, g_nsa], axis=1)
    return w.astype(BF16)


def _rope_tables(seq):
    inv_freq = ROPE_THETA ** (-jnp.arange(0, HEAD_DIM, 2, dtype=F32) / HEAD_DIM)
    ang = jnp.arange(seq, dtype=F32)[:, None] * inv_freq[None, :]
    cos, sin = jnp.cos(ang), jnp.sin(ang)
    return jnp.tile(cos, (1, 4)), jnp.tile(jnp.concatenate([-sin, sin], axis=1), (1, 2))


def _s5_kernel(u_ref, bre_ref, bim_ref, cre_ref, cim_ref, ar_ref, ai_ref, d_ref, wglu_ref,
               z_ref, xr_s, xi_s, str_s, sti_s, io_s):
    nb, ts, _ = u_ref.shape
    rows = ts * nb
    half_in = SSM_WIDTH // 2
    half_st = SSM_GROUPS * SSM_STATE // 2

    @pl.when(pl.program_id(0) == 0)
    def _():
        str_s[...] = jnp.zeros_like(str_s)
        sti_s[...] = jnp.zeros_like(sti_s)

    n_lc = SSM_WIDTH // LANES
    for b in range(nb):
        for lc in range(n_lc):
            io_s[lc, pl.ds(b, ts, stride=nb), :] = u_ref[b, :, lc * LANES:(lc + 1) * LANES]
    u2 = jnp.concatenate([io_s[lc] for lc in range(n_lc)], axis=1)
    ub = u2.astype(BF16)
    for hf in range(2):
        uh = ub[:, hf * half_in:(hf + 1) * half_in]
        xr_s[:, hf * half_st:(hf + 1) * half_st] = jnp.dot(uh, bre_ref[hf], preferred_element_type=F32)
        xi_s[:, hf * half_st:(hf + 1) * half_st] = jnp.dot(uh, bim_ref[hf], preferred_element_type=F32)

    for c in range(2 * half_st // S5_CHAN_CHUNK):
        sl = slice(c * S5_CHAN_CHUNK, (c + 1) * S5_CHAN_CHUNK)
        ar = jnp.broadcast_to(ar_ref[:, sl], (nb, S5_CHAN_CHUNK))
        ai = jnp.broadcast_to(ai_ref[:, sl], (nb, S5_CHAN_CHUNK))

        def step(t, carry, sl=sl, ar=ar, ai=ai):
            xr, xi = carry
            r0 = pl.multiple_of(t * nb, nb)
            nr = ar * xr - ai * xi + xr_s[pl.ds(r0, nb), sl]
            ni = ar * xi + ai * xr + xi_s[pl.ds(r0, nb), sl]
            xr_s[pl.ds(r0, nb), sl] = nr
            xi_s[pl.ds(r0, nb), sl] = ni
            return nr, ni

        xr, xi = lax.fori_loop(0, ts, step, (str_s[:, sl], sti_s[:, sl]), unroll=8)
        str_s[:, sl] = xr
        sti_s[:, sl] = xi

    ys = []
    for hf in range(2):
        st = slice(hf * half_st, (hf + 1) * half_st)
        ys.append(jnp.dot(xr_s[:, st].astype(BF16), cre_ref[hf], preferred_element_type=F32)
                  - jnp.dot(xi_s[:, st].astype(BF16), cim_ref[hf], preferred_element_type=F32))
    y = jnp.concatenate(ys, axis=1) + d_ref[...] * u2
    zz = jnp.dot(_gelu(y).astype(BF16), wglu_ref[...], preferred_element_type=F32)
    z = zz[:, :SSM_WIDTH] * jax.nn.sigmoid(zz[:, SSM_WIDTH:])
    for lc in range(n_lc):
        io_s[lc] = z[:, lc * LANES:(lc + 1) * LANES]
    for b in range(nb):
        for lc in range(n_lc):
            z_ref[b, :, lc * LANES:(lc + 1) * LANES] = io_s[lc, pl.ds(b, ts, stride=nb), :]


def _s5(u, p):
    nb, seq, _ = u.shape
    assert nb == SUBLANES, "the scan maps the batch onto the 8 sublanes"
    ts = S5_TIME_TILE
    nst = SSM_GROUPS * SSM_STATE
    return pl.pallas_call(
        _s5_kernel,
        grid=(seq // ts,),
        in_specs=[
            pl.BlockSpec((nb, ts, SSM_WIDTH), lambda i: (0, i, 0)),
            _const_spec(p["bre"].shape), _const_spec(p["bim"].shape),
            _const_spec(p["cre"].shape), _const_spec(p["cim"].shape),
            _const_spec((1, nst)), _const_spec((1, nst)),
            _const_spec((1, SSM_WIDTH)), _const_spec((SSM_WIDTH, 2 * SSM_WIDTH)),
        ],
        out_specs=pl.BlockSpec((nb, ts, SSM_WIDTH), lambda i: (0, i, 0)),
        out_shape=jax.ShapeDtypeStruct((nb, seq, SSM_WIDTH), F32),
        scratch_shapes=[
            pltpu.VMEM((ts * nb, nst), F32), pltpu.VMEM((ts * nb, nst), F32),
            pltpu.VMEM((nb, nst), F32), pltpu.VMEM((nb, nst), F32),
            pltpu.VMEM((SSM_WIDTH // LANES, ts * nb, LANES), F32),
        ],
        compiler_params=_cparams("arbitrary"),
        name="s5",
    )(u, p["bre"], p["bim"], p["cre"], p["cim"], p["ar"], p["ai"], p["d"], p["wglu"])


def _prep_s5(a_re, a_im, log_dt, b_re, b_im, c_re, c_im, d, w_glu):
    dt = jnp.exp(log_dt)[:, None]
    mag = jnp.exp(a_re * dt)
    ab_re = mag * jnp.cos(a_im * dt)
    ab_im = mag * jnp.sin(a_im * dt)
    den = a_re * a_re + a_im * a_im
    f_re = ((ab_re - 1.0) * a_re + ab_im * a_im) / den
    f_im = (ab_im * a_re - (ab_re - 1.0) * a_im) / den
    fb_re = f_re[:, :, None] * b_re - f_im[:, :, None] * b_im
    fb_im = f_re[:, :, None] * b_im + f_im[:, :, None] * b_re
    gh = SSM_GROUPS // 2
    eye = jnp.eye(gh, dtype=F32)

    def blk_in(w):
        w = w.reshape(2, gh, SSM_STATE, SSM_GROUP)
        return jnp.einsum("kgph,gj->kghjp", w, eye).reshape(2, gh * SSM_GROUP, gh * SSM_STATE).astype(BF16)

    def blk_out(w):
        w = w.reshape(2, gh, SSM_GROUP, SSM_STATE)
        return jnp.einsum("kghp,gj->kgpjh", w, eye).reshape(2, gh * SSM_STATE, gh * SSM_GROUP).astype(BF16)

    return dict(bre=blk_in(fb_re), bim=blk_in(fb_im), cre=blk_out(c_re), cim=blk_out(c_im),
                ar=ab_re.reshape(1, -1), ai=ab_im.reshape(1, -1), d=d.reshape(1, -1),
                wglu=w_glu.astype(BF16))


def _compress_kernel(kv_ref, pe_ref, w1_ref, b1_ref, w2_ref, o_ref, r_s):
    n = kv_ref.shape[1] // CMP_STRIDE
    half = CMP_STRIDE * HEAD_DIM
    lane = lax.broadcasted_iota(jnp.int32, (n, LANES), 1)
    for l in range(0, CMP_STRIDE, 2):
        x0 = kv_ref[0, pl.ds(l, n, stride=CMP_STRIDE), :]
        x1 = kv_ref[0, pl.ds(l + 1, n, stride=CMP_STRIDE), :]
        cols = slice(l * HEAD_DIM, (l + 2) * HEAD_DIM)
        r_s[0, :, cols] = jnp.where(lane < HEAD_DIM, x0, pltpu.roll(x1, HEAD_DIM, 1))
        r_s[1, :, cols] = jnp.where(lane < HEAD_DIM, pltpu.roll(x0, HEAD_DIM, 1), x1)
    for hh in range(N_KV_HEADS):
        r = r_s[hh]
        h1 = jnp.dot((r + pe_ref[0, 0:1, :]).astype(BF16), w1_ref[0, :half, :], preferred_element_type=F32)
        h2 = jnp.dot((r + pe_ref[0, 1:2, :]).astype(BF16), w1_ref[0, half:, :], preferred_element_type=F32)
        hid = h1 + pltpu.roll(h2, n - 1, 0) + b1_ref[0]
        out = jnp.dot(_gelu(hid).astype(BF16), w2_ref[0], preferred_element_type=F32)
        rowi = lax.broadcasted_iota(jnp.int32, out.shape, 0)
        o_ref[0, hh] = jnp.where(rowi < n - 1, out, 0.0)


def _compress(kvc, pe, w1, b1, w2):
    bsz, seq, _ = kvc.shape
    n = seq // CMP_STRIDE
    width = CMP_STRIDE * HEAD_DIM
    kv_sel = lambda b, c: (c, 0, 0)
    return pl.pallas_call(
        _compress_kernel,
        grid=(bsz, 2),
        in_specs=[
            pl.BlockSpec((1, seq, LANES), lambda b, c: (b, 0, c)),
            pl.BlockSpec((1, 2, width), kv_sel),
            pl.BlockSpec((1, 2 * width, CMP_HIDDEN), kv_sel),
            pl.BlockSpec((1, 1, CMP_HIDDEN), kv_sel),
            pl.BlockSpec((1, CMP_HIDDEN, HEAD_DIM), kv_sel),
        ],
        out_specs=pl.BlockSpec((1, N_KV_HEADS, n, HEAD_DIM), lambda b, c: (b, c, 0, 0)),
        out_shape=jax.ShapeDtypeStruct((bsz, 2 * N_KV_HEADS, n, HEAD_DIM), F32),
        scratch_shapes=[pltpu.VMEM((N_KV_HEADS, n, width), F32)],
        compiler_params=_cparams("parallel", "parallel"),
        name="compress",
    )(kvc, pe, w1, b1, w2)


def _sel_fold_matrix(n_cmp_pad, n_blk):
    ratio = SEL_BLOCK // CMP_STRIDE
    span = ratio + CMP_BLOCK // CMP_STRIDE - 1
    pad_l = CMP_BLOCK // CMP_STRIDE - 1
    i = np.arange(n_cmp_pad)[:, None]
    j = np.arange(n_blk)[None, :]
    lo = ratio * j - pad_l
    return jnp.asarray(((i >= lo) & (i < lo + span)), dtype=BF16)


def _softmax_pv_t(st, vt_tile, m_ref, acc_ref, cs, mask):
    if mask is not None:
        st = jnp.where(mask, st, NEG_INF)
    m_old = m_ref[:, cs]
    m_new = jnp.maximum(m_old, jnp.max(st, axis=0, keepdims=True))
    pt = jnp.exp2(st - m_new).astype(BF16)
    acc_ref[:, cs] = jnp.exp2(m_old - m_new) * acc_ref[:, cs] + jnp.dot(
        vt_tile, pt, preferred_element_type=F32)
    m_ref[:, cs] = m_new


def _nsa_t_kernel(qt_ref, kc_ref, vct_ref, selmt_ref, ks_ref, vst_ref, kw_ref, vwt_ref, gatet_ref,
                  o_ref, lhs_s, m_s, acc_s, mw_s, accw_s, ocmp_s, pg_s, scmp_s, stw_s, sta_s, stb_s):
    cols = qt_ref.shape[4]
    tq = cols // GQA_GROUP
    n_cmp = kc_ref.shape[2]
    n_blk = selmt_ref.shape[0]
    kpq = tq // KEY_TILE
    assert kpq == 2, "the selected-branch loop walks the key tiles in pairs"
    n_wt = WINDOW // KEY_TILE + kpq
    qi = pl.program_id(2)
    s0 = qi * tq
    wt0 = jnp.maximum(kpq * qi - WINDOW // KEY_TILE, 0)
    heads = [slice(g * tq, (g + 1) * tq) for g in range(GQA_GROUP)]

    lhs_s[0:LANES, :] = qt_ref[0, 0, 0]
    t_row = s0 + lax.broadcasted_iota(jnp.int32, (1, tq), 1)

    scmp_s[...] = jnp.dot(kc_ref[0, 0], lhs_s[0:LANES, :], preferred_element_type=F32)
    for i in range(n_wt):
        stw_s[i] = jnp.dot(kw_ref[0, 0, wt0 + i], lhs_s[0:LANES, :], preferred_element_type=F32)

    def cmp_branch(nk):
        cmp_end = lax.broadcasted_iota(jnp.int32, (nk, 1), 0) * CMP_STRIDE + (CMP_BLOCK - 1)
        cmask = cmp_end <= t_row
        for g, cs in enumerate(heads):
            s = jnp.where(cmask, scmp_s[0:nk, cs], NEG_INF)
            e = jnp.exp2(s - jnp.max(s, axis=0, keepdims=True))
            p = jnp.where(cmask, e, 0.0) * (1.0 / jnp.sum(e, axis=0, keepdims=True))
            ocmp_s[:, cs] = jnp.dot(vct_ref[0, 0, :, 0:nk], p.astype(BF16), preferred_element_type=F32)
            if g == 0:
                pg_s[0:nk, :] = p
            else:
                pg_s[0:nk, :] += p
        if nk < n_cmp:
            pg_s[nk:n_cmp, :] = jnp.zeros((n_cmp - nk, tq), F32)

    cmp_branch(n_cmp)

    pg = pg_s[...]
    selmt = selmt_ref[...]
    hi = pg.astype(BF16)
    r1 = pg - hi.astype(F32)
    mid = r1.astype(BF16)
    lo = (r1 - mid.astype(F32)).astype(BF16)
    bs = (jnp.dot(selmt, hi, preferred_element_type=F32) + jnp.dot(selmt, mid, preferred_element_type=F32)
          + jnp.dot(selmt, lo, preferred_element_type=F32))
    j_idx = lax.broadcasted_iota(jnp.int32, (n_blk, 1), 0)
    cur = t_row >> int(math.log2(SEL_BLOCK))
    forced = (j_idx == 0) | (j_idx == cur) | (j_idx == cur - 1)
    causal_b = j_idx * SEL_BLOCK <= t_row
    work = jnp.where(causal_b & ~forced, bs, NEG_INF)
    idx = lax.broadcasted_iota(jnp.int32, (n_blk, tq), 0).astype(F32)
    picked = jnp.where(forced, 1.0, 0.0)
    for _ in range(min(N_SEL, n_blk) - FORCED):
        mx = jnp.max(work, axis=0, keepdims=True)
        first = jnp.min(jnp.where(work == mx, idx, float(n_blk)), axis=0, keepdims=True)
        hit = idx == first
        picked = jnp.where(hit, 1.0, picked)
        work = jnp.where(hit, -jnp.inf, work)
    bias = jnp.where((picked > 0.0) & causal_b, 0.0, NEG_INF).astype(BF16)
    for cs in heads:
        lhs_s[LANES:2 * LANES, cs] = bias

    m_s[...] = jnp.full(m_s.shape, NEG_INF, F32)
    acc_s[...] = jnp.zeros(acc_s.shape, F32)
    mw_s[...] = jnp.full(mw_s.shape, NEG_INF, F32)
    accw_s[...] = jnp.zeros(accw_s.shape, F32)
    key_sub = lax.broadcasted_iota(jnp.int32, (KEY_TILE, 1), 0)

    for i in range(n_wt):
        pos = (wt0 + i) * KEY_TILE + key_sub
        mask = (pos <= t_row) & (pos > t_row - WINDOW)
        for cs in heads:
            _softmax_pv_t(stw_s[i, :, cs], vwt_ref[0, 0, wt0 + i], mw_s, accw_s, cs, mask)

    def produce(st_ref, kt):
        for cs in heads:
            st_ref[:, cs] = jnp.dot(ks_ref[0, 0, kt], lhs_s[:, cs], preferred_element_type=F32)

    def consume(st_ref, kt, diag):
        mask = (kt * KEY_TILE + key_sub <= t_row) if diag else None
        for cs in heads:
            _softmax_pv_t(st_ref[:, cs], vst_ref[0, 0, kt], m_s, acc_s, cs, mask)

    produce(sta_s, 0)

    def sel_pair(j, carry):
        kt = 2 * j
        produce(stb_s, kt + 1)
        consume(sta_s, kt, False)
        produce(sta_s, kt + 2)
        consume(stb_s, kt + 1, False)
        return carry

    lax.fori_loop(0, qi, sel_pair, 0)
    produce(stb_s, kpq * qi + 1)
    consume(sta_s, kpq * qi, True)
    consume(stb_s, kpq * qi + 1, True)

    gt = gatet_ref[0, 0]
    outs = []
    for g, cs in enumerate(heads):
        a_sel = acc_s[:, cs]
        a_win = accw_s[:, cs]
        o_t = (gt[g:g + 1, :] * ocmp_s[0:HEAD_DIM, cs]
               + gt[GQA_GROUP + g:GQA_GROUP + g + 1, :]
               * (1.0 / a_sel[HEAD_DIM:HEAD_DIM + 1, :]) * a_sel[0:HEAD_DIM, :]
               + gt[2 * GQA_GROUP + g:2 * GQA_GROUP + g + 1, :]
               * (1.0 / a_win[HEAD_DIM:HEAD_DIM + 1, :]) * a_win[0:HEAD_DIM, :])
        outs.append(o_t.T)
    o_ref[0] = jnp.concatenate(outs, axis=1)


def _nsa_t(qt, kc, vct, selmt, ks_a, vst, kw, vwt, gatet):
    bsz, _, nq, _, cols = qt.shape
    tq = cols // GQA_GROUP
    nt = ks_a.shape[2]
    n_cmp = kc.shape[2]
    bh4 = lambda b, h, i: (b, h, 0, 0)
    bh5 = lambda b, h, i: (b, h, 0, 0, 0)
    return pl.pallas_call(
        _nsa_t_kernel,
        grid=(bsz, N_KV_HEADS, nq),
        in_specs=[
            pl.BlockSpec((1, 1, 1, LANES, cols), lambda b, h, i: (b, h, i, 0, 0)),
            pl.BlockSpec((1, 1, n_cmp, LANES), bh4),
            pl.BlockSpec((1, 1, LANES, n_cmp), bh4),
            _const_spec(selmt.shape),
            pl.BlockSpec((1, 1, nt, KEY_TILE, 2 * LANES), bh5),
            pl.BlockSpec((1, 1, nt, V_ROWS, KEY_TILE), bh5),
            pl.BlockSpec((1, 1, nt, KEY_TILE, LANES), bh5),
            pl.BlockSpec((1, 1, nt, V_ROWS, KEY_TILE), bh5),
            pl.BlockSpec((1, 1, 2 * SUBLANES, tq), lambda b, h, i: (b, h, 0, i)),
        ],
        out_specs=pl.BlockSpec((1, tq, GQA_GROUP * HEAD_DIM), lambda b, h, i: (b, i, h)),
        out_shape=jax.ShapeDtypeStruct((bsz, nq * tq, Q_WIDTH), F32),
        scratch_shapes=[
            pltpu.VMEM((2 * LANES, cols), BF16),
            pltpu.VMEM((1, cols), F32), pltpu.VMEM((V_ROWS, cols), F32),
            pltpu.VMEM((1, cols), F32), pltpu.VMEM((V_ROWS, cols), F32),
            pltpu.VMEM((LANES, cols), F32),
            pltpu.VMEM((n_cmp, tq), F32),
            pltpu.VMEM((n_cmp, cols), F32),
            pltpu.VMEM((WINDOW // KEY_TILE + tq // KEY_TILE, KEY_TILE, cols), F32),
            pltpu.VMEM((KEY_TILE, cols), F32), pltpu.VMEM((KEY_TILE, cols), F32),
        ],
        compiler_params=_cparams("parallel", "parallel", "arbitrary"),
        name="nsa",
    )(qt, kc, vct, selmt, ks_a, vst, kw, vwt, gatet)


def _merge_kernel(x_ref, z_ref, a_ref, g_ref, wg_ref, wbs_ref, wbn_ref, wo_ref, o_ref, merged_s):
    x = x_ref[...]
    h = _rms(x, g_ref[...]).astype(BF16)
    z = z_ref[...].astype(BF16)
    a = a_ref[...].astype(BF16)
    for c in range(D_MODEL // MXU_DIM):
        cs = slice(c * MXU_DIM, (c + 1) * MXU_DIM)
        gs = slice(D_MODEL + c * MXU_DIM, D_MODEL + (c + 1) * MXU_DIM)
        sg_ssm = jax.nn.sigmoid(jnp.dot(h, wg_ref[:, cs], preferred_element_type=F32))
        sg_nsa = jax.nn.sigmoid(jnp.dot(h, wg_ref[:, gs], preferred_element_type=F32))
        merged_s[:, cs] = (sg_ssm * jnp.dot(z, wbs_ref[:, cs], preferred_element_type=F32)
                           + sg_nsa * jnp.dot(a, wbn_ref[:, cs], preferred_element_type=F32)).astype(BF16)
    o_ref[...] = x + jnp.dot(merged_s[...], wo_ref[...], preferred_element_type=F32)


def _merge(x2, z, a, g, wg, wbs, wbn, wo):
    t = x2.shape[0]
    tm = ROW_TILE
    row = lambda i: (i, 0)
    return pl.pallas_call(
        _merge_kernel,
        grid=(t // tm,),
        in_specs=[
            pl.BlockSpec((tm, D_MODEL), row), pl.BlockSpec((tm, SSM_WIDTH), row),
            pl.BlockSpec((tm, Q_WIDTH), row),
            _const_spec((1, D_MODEL)), _const_spec(wg.shape),
            _const_spec(wbs.shape), _const_spec(wbn.shape), _const_spec(wo.shape),
        ],
        out_specs=pl.BlockSpec((tm, D_MODEL), row),
        out_shape=jax.ShapeDtypeStruct((t, D_MODEL), F32),
        scratch_shapes=[pltpu.VMEM((tm, D_MODEL), BF16)],
        compiler_params=_cparams("parallel"),
        name="merge",
    )(x2, z, a, g, wg, wbs, wbn, wo)


def _ffn_kernel(x_ref, xh_ref, g_ref, wa_ref, wb_ref, cw_ref, cb_ref, wo_ref, gf_ref, o_ref, a_s, act_s,
                *, tiles_per_seq, final):
    tm = x_ref.shape[0]
    halo = xh_ref.shape[0]
    first = (pl.program_id(0) % tiles_per_seq) == 0
    x = x_ref[...]
    g = g_ref[...]
    h = _rms(x, g).astype(BF16)
    hh = _rms(xh_ref[...], g).astype(BF16)
    keep_halo = jnp.where(first, 0.0, 1.0)
    n_chunks = D_FF // FF_CHUNK
    y = x
    for c in range(n_chunks):
        cs = slice(c * FF_CHUNK, (c + 1) * FF_CHUNK)
        a_buf = a_s.at[c % 2]
        a_buf[0:halo, :] = jnp.dot(hh, wa_ref[:, cs], preferred_element_type=F32) * keep_halo
        a_buf[halo:halo + tm, :] = jnp.dot(h, wa_ref[:, cs], preferred_element_type=F32)
        b = jnp.dot(h, wb_ref[:, cs], preferred_element_type=F32)
        cw = cw_ref[:, cs]
        conv = cb_ref[:, cs]
        for k in range(CONV_WIDTH):
            off = halo - (CONV_WIDTH - 1) + k
            conv = conv + cw[k:k + 1, :] * a_buf[off:off + tm, :]
        act_s[:, cs] = (_gelu(conv) * b).astype(BF16)
        if (c + 1) % FF_GROUP == 0 or c + 1 == n_chunks:
            ks = slice((c // FF_GROUP) * FF_GROUP * FF_CHUNK, (c + 1) * FF_CHUNK)
            y = y + jnp.dot(act_s[:, ks], wo_ref[ks, :], preferred_element_type=F32)
    if final:
        y = _rms(y, gf_ref[...])
    o_ref[...] = y


def _ffn(x2, g, wa, wb, cw, cb, wo, gf, seq, final):
    t = x2.shape[0]
    tm = ROW_TILE
    halo = SUBLANES
    tps = seq // tm
    row = lambda i: (i, 0)
    return pl.pallas_call(
        functools.partial(_ffn_kernel, tiles_per_seq=tps, final=final),
        grid=(t // tm,),
        in_specs=[
            pl.BlockSpec((tm, D_MODEL), row),
            pl.BlockSpec((halo, D_MODEL), lambda i: (jnp.maximum(i * (tm // halo) - 1, 0), 0)),
            _const_spec((1, D_MODEL)),
            _const_spec(wa.shape), _const_spec(wb.shape),
            _const_spec(cw.shape), _const_spec(cb.shape), _const_spec(wo.shape),
            _const_spec((1, D_MODEL)),
        ],
        out_specs=pl.BlockSpec((tm, D_MODEL), row),
        out_shape=jax.ShapeDtypeStruct((t, D_MODEL), F32),
        scratch_shapes=[pltpu.VMEM((2, tm + halo, FF_CHUNK), F32), pltpu.VMEM((tm, D_FF), BF16)],
        compiler_params=_cparams("parallel"),
        name="ffn",
    )(x2, x2, g, wa, wb, cw, cb, wo, gf)


def _mixer_layer(x2, bsz, seq, cos, sin, norm_g, w_in, s5p, pe, w1, b1, w2, wbs, wbn, wo, selmt):
    u, qt, kvc, ks_a, kw, vst, vwt, gatet = _inproj(x2, norm_g, w_in[:, :_C_GSSM], cos, sin, seq)

    z = _s5(u.reshape(bsz, seq, SSM_WIDTH), s5p).reshape(bsz * seq, SSM_WIDTH)

    kvc_c = _compress(kvc.reshape(bsz, seq, 2 * KV_WIDTH), pe, w1, b1, w2)
    kc = jnp.pad(kvc_c[:, :N_KV_HEADS].astype(BF16), ((0, 0),) * 3 + ((0, LANES - HEAD_DIM),))
    vct = jnp.pad(kvc_c[:, N_KV_HEADS:].transpose(0, 1, 3, 2).astype(BF16),
                  ((0, 0), (0, 0), (0, LANES - HEAD_DIM), (0, 0)))
    a = _nsa_t(qt, kc, vct, selmt, ks_a, vst, kw, vwt, gatet).reshape(bsz * seq, Q_WIDTH)

    return _merge(x2, z, a, norm_g, w_in[:, _C_GSSM:], wbs, wbn, wo)


def kernel(x, norm_mix, w_in, ssm_a_re, ssm_a_im, ssm_log_dt, ssm_b_re, ssm_b_im, ssm_c_re, ssm_c_im,
           ssm_d, ssm_w_glu, cmp_pe_k, cmp_w1_k, cmp_b1_k, cmp_w2_k, cmp_pe_v, cmp_w1_v, cmp_b1_v,
           cmp_w2_v, w_branch_ssm, w_branch_nsa, w_out, norm_ffn, w_ffn_in, ffn_conv_w, ffn_conv_b,
           w_ffn_out, norm_final):
    bsz, seq, _ = x.shape
    depth = norm_mix.shape[0]
    assert seq % ROW_TILE == 0 and ROW_TILE % Q_TILE == 0 and seq >= WINDOW + Q_TILE
    cos, sin = _rope_tables(seq)
    assert N_SEL <= seq // SEL_BLOCK <= LANES, "selection blocks are mapped onto one vreg of lanes"
    selmt = _sel_fold_matrix(seq // CMP_STRIDE, LANES).T
    half = CMP_STRIDE * HEAD_DIM
    x2 = x.reshape(bsz * seq, D_MODEL)
    for l in range(depth):
        s5p = _prep_s5(ssm_a_re[l], ssm_a_im[l], ssm_log_dt[l], ssm_b_re[l], ssm_b_im[l],
                       ssm_c_re[l], ssm_c_im[l], ssm_d[l], ssm_w_glu[l])
        pe = jnp.stack([cmp_pe_k[l], cmp_pe_v[l]]).reshape(2, 2, half)
        w1 = jnp.stack([cmp_w1_k[l], cmp_w1_v[l]]).astype(BF16)
        b1 = jnp.stack([cmp_b1_k[l], cmp_b1_v[l]]).reshape(2, 1, CMP_HIDDEN)
        w2 = jnp.stack([cmp_w2_k[l], cmp_w2_v[l]]).astype(BF16)
        x2 = _mixer_layer(x2, bsz, seq, cos, sin, norm_mix[l].reshape(1, -1), _prep_w_in(w_in[l]), s5p,
                          pe, w1, b1, w2, w_branch_ssm[l].astype(BF16), w_branch_nsa[l].astype(BF16),
                          w_out[l].astype(BF16), selmt)
        wf = w_ffn_in[l].astype(BF16)
        x2 = _ffn(x2, norm_ffn[l].reshape(1, -1), wf[:, :D_FF], wf[:, D_FF:], ffn_conv_w[l],
                  ffn_conv_b[l].reshape(1, -1), w_ffn_out[l].astype(BF16), norm_final.reshape(1, -1),
                  seq, final=(l == depth - 1))
    return x2.reshape(bsz, seq, D_MODEL)
```

```python
import functools
import math

import numpy as np
import jax
import jax.numpy as jnp
from jax import lax
from jax.experimental import pallas as pl
from jax.experimental.pallas import tpu as pltpu

F32 = jnp.float32
BF16 = jnp.bfloat16

D_MODEL = 1024
SSM_WIDTH = 512
SSM_GROUP = 16
SSM_GROUPS = 32
SSM_STATE = 64
N_Q_HEADS = 8
N_KV_HEADS = 2
HEAD_DIM = 64
GQA_GROUP = 4
CMP_BLOCK = 32
CMP_STRIDE = 16
CMP_HIDDEN = 256
SEL_BLOCK = 64
N_SEL = 16
WINDOW = 512
N_NSA_BRANCH = 3
D_FF = 2816
CONV_WIDTH = 3
RMS_EPS = 1e-6
NEG_INF = -1e30
FORCED = 3
ROPE_THETA = 10000.0
Q_WIDTH = N_Q_HEADS * HEAD_DIM
KV_WIDTH = N_KV_HEADS * HEAD_DIM

LANES = 128
SUBLANES = 8
MXU_DIM = 256
VMEM_LIMIT = 56 * 1024 * 1024

ROW_TILE = 512
S5_TIME_TILE = 64
S5_CHAN_CHUNK = 1024
KEY_TILE = MXU_DIM
Q_TILE = 2 * KEY_TILE
FF_CHUNK = MXU_DIM
FF_GROUP = D_FF // FF_CHUNK
Q_SCALE = HEAD_DIM ** -0.5 * math.log2(math.e)
V_ROWS = 80

_C_U = 0
_C_Q = _C_U + SSM_WIDTH
_C_KCMP = _C_Q + Q_WIDTH
_C_VCMP = _C_KCMP + KV_WIDTH
_C_KSEL = _C_VCMP + KV_WIDTH
_C_KWIN = _C_KSEL + KV_WIDTH
_C_VSEL = _C_KWIN + KV_WIDTH
_C_VWIN = _C_VSEL + KV_WIDTH
_C_GATE = _C_VWIN + KV_WIDTH
_C_GSSM = _C_GATE + LANES
_C_GNSA = _C_GSSM + D_MODEL


def _cparams(*sem):
    return pltpu.CompilerParams(dimension_semantics=sem, vmem_limit_bytes=VMEM_LIMIT)


def _gelu(x):
    return x * (0.5 * (1.0 + jnp.tanh(math.sqrt(2.0 / math.pi) * (x + 0.044715 * (x * x * x)))))


def _rms(x, g):
    ms = jnp.mean(x * x, axis=-1, keepdims=True)
    return (x * lax.rsqrt(ms + RMS_EPS)) * g


def _const_spec(shape):
    nd = len(shape)
    return pl.BlockSpec(shape, lambda *_: (0,) * nd)


def _inproj_kernel(x_ref, g_ref, w_ref, cos_ref, sin_ref,
                   u_ref, qt_ref, kvc_ref, ks_ref, kw_ref, vst_ref, vwt_ref, gatet_ref,
                   *, tiles_per_seq):
    tm = x_ref.shape[0]
    tk = KEY_TILE
    n_sub = tm // tk
    tile0 = (pl.program_id(0) % tiles_per_seq) * n_sub
    h = _rms(x_ref[...], g_ref[...]).astype(BF16)
    lane_k = lax.broadcasted_iota(jnp.int32, (tk, LANES), 1)
    row_k = lax.broadcasted_iota(jnp.int32, (tk, LANES), 0)

    def split_heads(a, j):
        piece = a[j * tk:(j + 1) * tk, :]
        return (jnp.where(lane_k < HEAD_DIM, piece, 0.0),
                jnp.where(lane_k < HEAD_DIM, pltpu.roll(piece, HEAD_DIM, 1), 0.0))

    def put_values_t(v, out_ref):
        vt = v.T.astype(BF16)
        tail = jnp.where(lax.broadcasted_iota(jnp.int32, (V_ROWS - HEAD_DIM, tk), 0) == 0, 1.0, 0.0)
        for hh in range(N_KV_HEADS):
            for j in range(n_sub):
                out_ref[0, hh, j, 0:HEAD_DIM, :] = vt[hh * HEAD_DIM:(hh + 1) * HEAD_DIM, j * tk:(j + 1) * tk]
                out_ref[0, hh, j, HEAD_DIM:V_ROWS, :] = tail.astype(BF16)

    def proj(c0, n):
        return jnp.dot(h, w_ref[:, c0:c0 + n], preferred_element_type=F32)

    cos = cos_ref[...]
    sin = sin_ref[...]
    lane = lax.broadcasted_iota(jnp.int32, (tm, LANES), 1)
    first_half = (lane & (HEAD_DIM // 2)) == 0

    def rope(v):
        swapped = jnp.where(first_half, pltpu.roll(v, LANES - HEAD_DIM // 2, 1),
                            pltpu.roll(v, HEAD_DIM // 2, 1))
        return v * cos + swapped * sin

    u_ref[...] = proj(_C_U, SSM_WIDTH)

    tq = Q_TILE
    qt_ref[0, :, :, HEAD_DIM:LANES, :] = jnp.zeros(
        (N_KV_HEADS, tm // tq, LANES - HEAD_DIM, GQA_GROUP * tq), BF16)
    for c in range(Q_WIDTH // LANES):
        if c % 2 == 0:
            q_pair = proj(_C_Q + c * LANES, 2 * LANES)
        q_c = q_pair[:, (c % 2) * LANES:(c % 2 + 1) * LANES]
        qt = (rope(q_c) * Q_SCALE).T.astype(BF16)
        for hl in range(2):
            head = 2 * c + hl
            hkv, g = head // GQA_GROUP, head % GQA_GROUP
            for j in range(tm // tq):
                qt_ref[0, hkv, j, 0:HEAD_DIM, g * tq:(g + 1) * tq] = (
                    qt[hl * HEAD_DIM:(hl + 1) * HEAD_DIM, j * tq:(j + 1) * tq])

    kv_cmp = proj(_C_KCMP, 2 * LANES)
    kvc_ref[:, 0:LANES] = rope(kv_cmp[:, 0:LANES])
    kvc_ref[:, LANES:2 * LANES] = kv_cmp[:, LANES:2 * LANES]

    k_pair = proj(_C_KSEL, 2 * LANES)
    k_sel = rope(k_pair[:, 0:LANES])
    k_win = rope(k_pair[:, LANES:2 * LANES])
    for j in range(n_sub):
        onehot = jnp.where(((tile0 + j) * tk + row_k) >> int(math.log2(SEL_BLOCK)) == lane_k, 1.0, 0.0)
        for hh, (ks_h, kw_h) in enumerate(zip(split_heads(k_sel, j), split_heads(k_win, j))):
            ks_ref[0, hh, j, :, 0:LANES] = ks_h.astype(BF16)
            ks_ref[0, hh, j, :, LANES:2 * LANES] = onehot.astype(BF16)
            kw_ref[0, hh, j] = kw_h.astype(BF16)
    v_pair = proj(_C_VSEL, 2 * LANES)
    put_values_t(v_pair[:, 0:LANES], vst_ref)
    put_values_t(v_pair[:, LANES:2 * LANES], vwt_ref)

    gate_t = jax.nn.sigmoid(proj(_C_GATE, LANES)).T
    for hh in range(N_KV_HEADS):
        gatet_ref[0, hh] = gate_t[hh * HEAD_DIM:hh * HEAD_DIM + 2 * SUBLANES, :]


def _inproj(x2, g, w, cos, sin, seq):
    t = x2.shape[0]
    bsz = t // seq
    tm = ROW_TILE
    spt = seq // tm
    tk = KEY_TILE
    n_sub = tm // tk
    nt = seq // tk
    row = lambda i: (i, 0)
    per_tile = lambda i: (i // spt, 0, i % spt, 0, 0)
    rows_out = lambda n, dt: (jax.ShapeDtypeStruct((t, n), dt), pl.BlockSpec((tm, n), row))
    tile_out = lambda r, c: (jax.ShapeDtypeStruct((bsz, N_KV_HEADS, nt, r, c), BF16),
                             pl.BlockSpec((1, N_KV_HEADS, n_sub, r, c), per_tile))
    outs, out_specs = zip(
        rows_out(SSM_WIDTH, F32),
        (jax.ShapeDtypeStruct((bsz, N_KV_HEADS, seq // Q_TILE, LANES, GQA_GROUP * Q_TILE), BF16),
         pl.BlockSpec((1, N_KV_HEADS, tm // Q_TILE, LANES, GQA_GROUP * Q_TILE), per_tile)),
        rows_out(2 * KV_WIDTH, F32),
        tile_out(tk, 2 * LANES),
        tile_out(tk, LANES),
        tile_out(V_ROWS, tk),
        tile_out(V_ROWS, tk),
        (jax.ShapeDtypeStruct((bsz, N_KV_HEADS, 2 * SUBLANES, seq), F32),
         pl.BlockSpec((1, N_KV_HEADS, 2 * SUBLANES, tm), lambda i: (i // spt, 0, 0, i % spt))),
    )
    return pl.pallas_call(
        functools.partial(_inproj_kernel, tiles_per_seq=spt),
        grid=(t // tm,),
        in_specs=[
            pl.BlockSpec((tm, D_MODEL), row),
            _const_spec((1, D_MODEL)),
            _const_spec((D_MODEL, _C_GSSM)),
            pl.BlockSpec((tm, LANES), lambda i: (i % spt, 0)),
            pl.BlockSpec((tm, LANES), lambda i: (i % spt, 0)),
        ],
        out_specs=list(out_specs),
        out_shape=list(outs),
        compiler_params=_cparams("parallel"),
        name="inproj",
    )(x2, g, w, cos, sin)


def _prep_w_in(w_in):
    o = np.cumsum((0, SSM_WIDTH, Q_WIDTH) + (KV_WIDTH,) * 6 + (N_Q_HEADS * N_NSA_BRANCH, D_MODEL, D_MODEL))
    u, q, k_cmp, v_cmp, k_sel, v_sel, k_win, v_win, gate, g_ssm, g_nsa = (
        w_in[:, o[i]:o[i + 1]] for i in range(11))
    gate = gate.reshape(D_MODEL, N_KV_HEADS, GQA_GROUP, N_NSA_BRANCH).transpose(0, 1, 3, 2)
    gate = gate.reshape(D_MODEL, N_KV_HEADS, N_NSA_BRANCH * GQA_GROUP)
    gate = jnp.pad(gate, ((0, 0), (0, 0), (0, HEAD_DIM - N_NSA_BRANCH * GQA_GROUP))).reshape(D_MODEL, LANES)
    w = jnp.concatenate([u, q, k_cmp, v_cmp, k_sel, k_win, v_sel, v_win, gate, g_ssm, g_nsa], axis=1)
    return w.astype(BF16)


def _rope_tables(seq):
    inv_freq = ROPE_THETA ** (-jnp.arange(0, HEAD_DIM, 2, dtype=F32) / HEAD_DIM)
    ang = jnp.arange(seq, dtype=F32)[:, None] * inv_freq[None, :]
    cos, sin = jnp.cos(ang), jnp.sin(ang)
    return jnp.tile(cos, (1, 4)), jnp.tile(jnp.concatenate([-sin, sin], axis=1), (1, 2))


def _s5_kernel(u_ref, bre_ref, bim_ref, cre_ref, cim_ref, ar_ref, ai_ref, d_ref, wglu_ref,
               z_ref, xr_s, xi_s, str_s, sti_s, io_s):
    nb, ts, _ = u_ref.shape
    rows = ts * nb
    half_in = SSM_WIDTH // 2
    half_st = SSM_GROUPS * SSM_STATE // 2

    @pl.when(pl.program_id(0) == 0)
    def _():
        str_s[...] = jnp.zeros_like(str_s)
        sti_s[...] = jnp.zeros_like(sti_s)

    n_lc = SSM_WIDTH // LANES
    for b in range(nb):
        for lc in range(n_lc):
            io_s[lc, pl.ds(b, ts, stride=nb), :] = u_ref[b, :, lc * LANES:(lc + 1) * LANES]
    u2 = jnp.concatenate([io_s[lc] for lc in range(n_lc)], axis=1)
    ub = u2.astype(BF16)
    for hf in range(2):
        uh = ub[:, hf * half_in:(hf + 1) * half_in]
        xr_s[:, hf * half_st:(hf + 1) * half_st] = jnp.dot(uh, bre_ref[hf], preferred_element_type=F32)
        xi_s[:, hf * half_st:(hf + 1) * half_st] = jnp.dot(uh, bim_ref[hf], preferred_element_type=F32)

    for c in range(2 * half_st // S5_CHAN_CHUNK):
        sl = slice(c * S5_CHAN_CHUNK, (c + 1) * S5_CHAN_CHUNK)
        ar = jnp.broadcast_to(ar_ref[:, sl], (nb, S5_CHAN_CHUNK))
        ai = jnp.broadcast_to(ai_ref[:, sl], (nb, S5_CHAN_CHUNK))

        def step(t, carry, sl=sl, ar=ar, ai=ai):
            xr, xi = carry
            r0 = pl.multiple_of(t * nb, nb)
            nr = ar * xr - ai * xi + xr_s[pl.ds(r0, nb), sl]
            ni = ar * xi + ai * xr + xi_s[pl.ds(r0, nb), sl]
            xr_s[pl.ds(r0, nb), sl] = nr
            xi_s[pl.ds(r0, nb), sl] = ni
            return nr, ni

        xr, xi = lax.fori_loop(0, ts, step, (str_s[:, sl], sti_s[:, sl]), unroll=8)
        str_s[:, sl] = xr
        sti_s[:, sl] = xi

    ys = []
    for hf in range(2):
        st = slice(hf * half_st, (hf + 1) * half_st)
        ys.append(jnp.dot(xr_s[:, st].astype(BF16), cre_ref[hf], preferred_element_type=F32)
                  - jnp.dot(xi_s[:, st].astype(BF16), cim_ref[hf], preferred_element_type=F32))
    y = jnp.concatenate(ys, axis=1) + d_ref[...] * u2
    zz = jnp.dot(_gelu(y).astype(BF16), wglu_ref[...], preferred_element_type=F32)
    z = zz[:, :SSM_WIDTH] * jax.nn.sigmoid(zz[:, SSM_WIDTH:])
    for lc in range(n_lc):
        io_s[lc] = z[:, lc * LANES:(lc + 1) * LANES]
    for b in range(nb):
        for lc in range(n_lc):
            z_ref[b, :, lc * LANES:(lc + 1) * LANES] = io_s[lc, pl.ds(b, ts, stride=nb), :]


def _s5(u, p):
    nb, seq, _ = u.shape
    assert nb == SUBLANES, "the scan maps the batch onto the 8 sublanes"
    ts = S5_TIME_TILE
    nst = SSM_GROUPS * SSM_STATE
    return pl.pallas_call(
        _s5_kernel,
        grid=(seq // ts,),
        in_specs=[
            pl.BlockSpec((nb, ts, SSM_WIDTH), lambda i: (0, i, 0)),
            _const_spec(p["bre"].shape), _const_spec(p["bim"].shape),
            _const_spec(p["cre"].shape), _const_spec(p["cim"].shape),
            _const_spec((1, nst)), _const_spec((1, nst)),
            _const_spec((1, SSM_WIDTH)), _const_spec((SSM_WIDTH, 2 * SSM_WIDTH)),
        ],
        out_specs=pl.BlockSpec((nb, ts, SSM_WIDTH), lambda i: (0, i, 0)),
        out_shape=jax.ShapeDtypeStruct((nb, seq, SSM_WIDTH), F32),
        scratch_shapes=[
            pltpu.VMEM((ts * nb, nst), F32), pltpu.VMEM((ts * nb, nst), F32),
            pltpu.VMEM((nb, nst), F32), pltpu.VMEM((nb, nst), F32),
            pltpu.VMEM((SSM_WIDTH // LANES, ts * nb, LANES), F32),
        ],
        compiler_params=_cparams("arbitrary"),
        name="s5",
    )(u, p["bre"], p["bim"], p["cre"], p["cim"], p["ar"], p["ai"], p["d"], p["wglu"])


def _prep_s5(a_re, a_im, log_dt, b_re, b_im, c_re, c_im, d, w_glu):
    dt = jnp.exp(log_dt)[:, None]
    mag = jnp.exp(a_re * dt)
    ab_re = mag * jnp.cos(a_im * dt)
    ab_im = mag * jnp.sin(a_im * dt)
    den = a_re * a_re + a_im * a_im
    f_re = ((ab_re - 1.0) * a_re + ab_im * a_im) / den
    f_im = (ab_im * a_re - (ab_re - 1.0) * a_im) / den
    fb_re = f_re[:, :, None] * b_re - f_im[:, :, None] * b_im
    fb_im = f_re[:, :, None] * b_im + f_im[:, :, None] * b_re
    gh = SSM_GROUPS // 2
    eye = jnp.eye(gh, dtype=F32)

    def blk_in(w):
        w = w.reshape(2, gh, SSM_STATE, SSM_GROUP)
        return jnp.einsum("kgph,gj->kghjp", w, eye).reshape(2, gh * SSM_GROUP, gh * SSM_STATE).astype(BF16)

    def blk_out(w):
        w = w.reshape(2, gh, SSM_GROUP, SSM_STATE)
        return jnp.einsum("kghp,gj->kgpjh", w, eye).reshape(2, gh * SSM_STATE, gh * SSM_GROUP).astype(BF16)

    return dict(bre=blk_in(fb_re), bim=blk_in(fb_im), cre=blk_out(c_re), cim=blk_out(c_im),
                ar=ab_re.reshape(1, -1), ai=ab_im.reshape(1, -1), d=d.reshape(1, -1),
                wglu=w_glu.astype(BF16))


def _compress_kernel(kv_ref, pe_ref, w1_ref, b1_ref, w2_ref, o_ref, r_s):
    n = kv_ref.shape[1] // CMP_STRIDE
    half = CMP_STRIDE * HEAD_DIM
    lane = lax.broadcasted_iota(jnp.int32, (n, LANES), 1)
    for l in range(0, CMP_STRIDE, 2):
        x0 = kv_ref[0, pl.ds(l, n, stride=CMP_STRIDE), :]
        x1 = kv_ref[0, pl.ds(l + 1, n, stride=CMP_STRIDE), :]
        cols = slice(l * HEAD_DIM, (l + 2) * HEAD_DIM)
        r_s[0, :, cols] = jnp.where(lane < HEAD_DIM, x0, pltpu.roll(x1, HEAD_DIM, 1))
        r_s[1, :, cols] = jnp.where(lane < HEAD_DIM, pltpu.roll(x0, HEAD_DIM, 1), x1)
    for hh in range(N_KV_HEADS):
        r = r_s[hh]
        h1 = jnp.dot((r + pe_ref[0, 0:1, :]).astype(BF16), w1_ref[0, :half, :], preferred_element_type=F32)
        h2 = jnp.dot((r + pe_ref[0, 1:2, :]).astype(BF16), w1_ref[0, half:, :], preferred_element_type=F32)
        hid = h1 + pltpu.roll(h2, n - 1, 0) + b1_ref[0]
        out = jnp.dot(_gelu(hid).astype(BF16), w2_ref[0], preferred_element_type=F32)
        rowi = lax.broadcasted_iota(jnp.int32, out.shape, 0)
        o_ref[0, hh] = jnp.where(rowi < n - 1, out, 0.0)


def _compress(kvc, pe, w1, b1, w2):
    bsz, seq, _ = kvc.shape
    n = seq // CMP_STRIDE
    width = CMP_STRIDE * HEAD_DIM
    kv_sel = lambda b, c: (c, 0, 0)
    return pl.pallas_call(
        _compress_kernel,
        grid=(bsz, 2),
        in_specs=[
            pl.BlockSpec((1, seq, LANES), lambda b, c: (b, 0, c)),
            pl.BlockSpec((1, 2, width), kv_sel),
            pl.BlockSpec((1, 2 * width, CMP_HIDDEN), kv_sel),
            pl.BlockSpec((1, 1, CMP_HIDDEN), kv_sel),
            pl.BlockSpec((1, CMP_HIDDEN, HEAD_DIM), kv_sel),
        ],
        out_specs=pl.BlockSpec((1, N_KV_HEADS, n, HEAD_DIM), lambda b, c: (b, c, 0, 0)),
        out_shape=jax.ShapeDtypeStruct((bsz, 2 * N_KV_HEADS, n, HEAD_DIM), F32),
        scratch_shapes=[pltpu.VMEM((N_KV_HEADS, n, width), F32)],
        compiler_params=_cparams("parallel", "parallel"),
        name="compress",
    )(kvc, pe, w1, b1, w2)


def _sel_fold_matrix(n_cmp_pad, n_blk):
    ratio = SEL_BLOCK // CMP_STRIDE
    span = ratio + CMP_BLOCK // CMP_STRIDE - 1
    pad_l = CMP_BLOCK // CMP_STRIDE - 1
    i = np.arange(n_cmp_pad)[:, None]
    j = np.arange(n_blk)[None, :]
    lo = ratio * j - pad_l
    return jnp.asarray(((i >= lo) & (i < lo + span)), dtype=BF16)


def _softmax_pv_t(st, vt_tile, m_ref, acc_ref, cs, mask):
    if mask is not None:
        st = jnp.where(mask, st, NEG_INF)
    m_old = m_ref[:, cs]
    m_new = jnp.maximum(m_old, jnp.max(st, axis=0, keepdims=True))
    pt = jnp.exp2(st - m_new).astype(BF16)
    acc_ref[:, cs] = jnp.exp2(m_old - m_new) * acc_ref[:, cs] + jnp.dot(
        vt_tile, pt, preferred_element_type=F32)
    m_ref[:, cs] = m_new


def _nsa_t_kernel(qt_ref, kc_ref, vct_ref, selmt_ref, ks_ref, vst_ref, kw_ref, vwt_ref, gatet_ref,
                  o_ref, lhs_s, m_s, acc_s, mw_s, accw_s, ocmp_s, pg_s, scmp_s, stw_s, sta_s, stb_s):
    cols = qt_ref.shape[4]
    tq = cols // GQA_GROUP
    n_cmp = kc_ref.shape[2]
    n_blk = selmt_ref.shape[0]
    kpq = tq // KEY_TILE
    assert kpq == 2, "the selected-branch loop walks the key tiles in pairs"
    n_wt = WINDOW // KEY_TILE + kpq
    qi = pl.program_id(2)
    s0 = qi * tq
    wt0 = jnp.maximum(kpq * qi - WINDOW // KEY_TILE, 0)
    heads = [slice(g * tq, (g + 1) * tq) for g in range(GQA_GROUP)]

    lhs_s[0:LANES, :] = qt_ref[0, 0, 0]
    t_row = s0 + lax.broadcasted_iota(jnp.int32, (1, tq), 1)

    scmp_s[...] = jnp.dot(kc_ref[0, 0], lhs_s[0:LANES, :], preferred_element_type=F32)
    for i in range(n_wt):
        stw_s[i] = jnp.dot(kw_ref[0, 0, wt0 + i], lhs_s[0:LANES, :], preferred_element_type=F32)

    def cmp_branch(nk):
        cmp_end = lax.broadcasted_iota(jnp.int32, (nk, 1), 0) * CMP_STRIDE + (CMP_BLOCK - 1)
        cmask = cmp_end <= t_row
        for g, cs in enumerate(heads):
            s = jnp.where(cmask, scmp_s[0:nk, cs], NEG_INF)
            e = jnp.exp2(s - jnp.max(s, axis=0, keepdims=True))
            p = jnp.where(cmask, e, 0.0) * (1.0 / jnp.sum(e, axis=0, keepdims=True))
            ocmp_s[:, cs] = jnp.dot(vct_ref[0, 0, :, 0:nk], p.astype(BF16), preferred_element_type=F32)
            if g == 0:
                pg_s[0:nk, :] = p
            else:
                pg_s[0:nk, :] += p
        if nk < n_cmp:
            pg_s[nk:n_cmp, :] = jnp.zeros((n_cmp - nk, tq), F32)

    cmp_branch(n_cmp)

    pg = pg_s[...]
    selmt = selmt_ref[...]
    hi = pg.astype(BF16)
    r1 = pg - hi.astype(F32)
    mid = r1.astype(BF16)
    lo = (r1 - mid.astype(F32)).astype(BF16)
    bs = (jnp.dot(selmt, hi, preferred_element_type=F32) + jnp.dot(selmt, mid, preferred_element_type=F32)
          + jnp.dot(selmt, lo, preferred_element_type=F32))
    j_idx = lax.broadcasted_iota(jnp.int32, (n_blk, 1), 0)
    cur = t_row >> int(math.log2(SEL_BLOCK))
    forced = (j_idx == 0) | (j_idx == cur) | (j_idx == cur - 1)
    causal_b = j_idx * SEL_BLOCK <= t_row
    work = jnp.where(causal_b & ~forced, bs, NEG_INF)
    idx = lax.broadcasted_iota(jnp.int32, (n_blk, tq), 0).astype(F32)
    picked = jnp.where(forced, 1.0, 0.0)
    for _ in range(min(N_SEL, n_blk) - FORCED):
        mx = jnp.max(work, axis=0, keepdims=True)
        first = jnp.min(jnp.where(work == mx, idx, float(n_blk)), axis=0, keepdims=True)
        hit = idx == first
        picked = jnp.where(hit, 1.0, picked)
        work = jnp.where(hit, -jnp.inf, work)
    bias = jnp.where((picked > 0.0) & causal_b, 0.0, NEG_INF).astype(BF16)
    for cs in heads:
        lhs_s[LANES:2 * LANES, cs] = bias

    m_s[...] = jnp.full(m_s.shape, NEG_INF, F32)
    acc_s[...] = jnp.zeros(acc_s.shape, F32)
    mw_s[...] = jnp.full(mw_s.shape, NEG_INF, F32)
    accw_s[...] = jnp.zeros(accw_s.shape, F32)
    key_sub = lax.broadcasted_iota(jnp.int32, (KEY_TILE, 1), 0)

    for i in range(n_wt):
        pos = (wt0 + i) * KEY_TILE + key_sub
        mask = (pos <= t_row) & (pos > t_row - WINDOW)
        for cs in heads:
            _softmax_pv_t(stw_s[i, :, cs], vwt_ref[0, 0, wt0 + i], mw_s, accw_s, cs, mask)

    def produce(st_ref, kt):
        for cs in heads:
            st_ref[:, cs] = jnp.dot(ks_ref[0, 0, kt], lhs_s[:, cs], preferred_element_type=F32)

    def consume(st_ref, kt, diag):
        mask = (kt * KEY_TILE + key_sub <= t_row) if diag else None
        for cs in heads:
            _softmax_pv_t(st_ref[:, cs], vst_ref[0, 0, kt], m_s, acc_s, cs, mask)

    produce(sta_s, 0)

    def sel_pair(j, carry):
        kt = 2 * j
        produce(stb_s, kt + 1)
        consume(sta_s, kt, False)
        produce(sta_s, kt + 2)
        consume(stb_s, kt + 1, False)
        return carry

    lax.fori_loop(0, qi, sel_pair, 0)
    produce(stb_s, kpq * qi + 1)
    consume(sta_s, kpq * qi, True)
    consume(stb_s, kpq * qi + 1, True)

    gt = gatet_ref[0, 0]
    outs = []
    for g, cs in enumerate(heads):
        a_sel = acc_s[:, cs]
        a_win = accw_s[:, cs]
        o_t = (gt[g:g + 1, :] * ocmp_s[0:HEAD_DIM, cs]
               + gt[GQA_GROUP + g:GQA_GROUP + g + 1, :]
               * (1.0 / a_sel[HEAD_DIM:HEAD_DIM + 1, :]) * a_sel[0:HEAD_DIM, :]
               + gt[2 * GQA_GROUP + g:2 * GQA_GROUP + g + 1, :]
               * (1.0 / a_win[HEAD_DIM:HEAD_DIM + 1, :]) * a_win[0:HEAD_DIM, :])
        outs.append(o_t.T)
    o_ref[0] = jnp.concatenate(outs, axis=1)


def _nsa_t(qt, kc, vct, selmt, ks_a, vst, kw, vwt, gatet):
    bsz, _, nq, _, cols = qt.shape
    tq = cols // GQA_GROUP
    nt = ks_a.shape[2]
    n_cmp = kc.shape[2]
    bh4 = lambda b, h, i: (b, h, 0, 0)
    bh5 = lambda b, h, i: (b, h, 0, 0, 0)
    return pl.pallas_call(
        _nsa_t_kernel,
        grid=(bsz, N_KV_HEADS, nq),
        in_specs=[
            pl.BlockSpec((1, 1, 1, LANES, cols), lambda b, h, i: (b, h, i, 0, 0)),
            pl.BlockSpec((1, 1, n_cmp, LANES), bh4),
            pl.BlockSpec((1, 1, LANES, n_cmp), bh4),
            _const_spec(selmt.shape),
            pl.BlockSpec((1, 1, nt, KEY_TILE, 2 * LANES), bh5),
            pl.BlockSpec((1, 1, nt, V_ROWS, KEY_TILE), bh5),
            pl.BlockSpec((1, 1, nt, KEY_TILE, LANES), bh5),
            pl.BlockSpec((1, 1, nt, V_ROWS, KEY_TILE), bh5),
            pl.BlockSpec((1, 1, 2 * SUBLANES, tq), lambda b, h, i: (b, h, 0, i)),
        ],
        out_specs=pl.BlockSpec((1, tq, GQA_GROUP * HEAD_DIM), lambda b, h, i: (b, i, h)),
        out_shape=jax.ShapeDtypeStruct((bsz, nq * tq, Q_WIDTH), F32),
        scratch_shapes=[
            pltpu.VMEM((2 * LANES, cols), BF16),
            pltpu.VMEM((1, cols), F32), pltpu.VMEM((V_ROWS, cols), F32),
            pltpu.VMEM((1, cols), F32), pltpu.VMEM((V_ROWS, cols), F32),
            pltpu.VMEM((LANES, cols), F32),
            pltpu.VMEM((n_cmp, tq), F32),
            pltpu.VMEM((n_cmp, cols), F32),
            pltpu.VMEM((WINDOW // KEY_TILE + tq // KEY_TILE, KEY_TILE, cols), F32),
            pltpu.VMEM((KEY_TILE, cols), F32), pltpu.VMEM((KEY_TILE, cols), F32),
        ],
        compiler_params=_cparams("parallel", "parallel", "arbitrary"),
        name="nsa",
    )(qt, kc, vct, selmt, ks_a, vst, kw, vwt, gatet)


def _merge_kernel(x_ref, z_ref, a_ref, g_ref, wg_ref, wbs_ref, wbn_ref, wo_ref, o_ref, merged_s):
    x = x_ref[...]
    h = _rms(x, g_ref[...]).astype(BF16)
    z = z_ref[...].astype(BF16)
    a = a_ref[...].astype(BF16)
    for c in range(D_MODEL // MXU_DIM):
        cs = slice(c * MXU_DIM, (c + 1) * MXU_DIM)
        gs = slice(D_MODEL + c * MXU_DIM, D_MODEL + (c + 1) * MXU_DIM)
        sg_ssm = jax.nn.sigmoid(jnp.dot(h, wg_ref[:, cs], preferred_element_type=F32))
        sg_nsa = jax.nn.sigmoid(jnp.dot(h, wg_ref[:, gs], preferred_element_type=F32))
        merged_s[:, cs] = (sg_ssm * jnp.dot(z, wbs_ref[:, cs], preferred_element_type=F32)
                           + sg_nsa * jnp.dot(a, wbn_ref[:, cs], preferred_element_type=F32)).astype(BF16)
    o_ref[...] = x + jnp.dot(merged_s[...], wo_ref[...], preferred_element_type=F32)


def _merge(x2, z, a, g, wg, wbs, wbn, wo):
    t = x2.shape[0]
    tm = ROW_TILE
    row = lambda i: (i, 0)
    return pl.pallas_call(
        _merge_kernel,
        grid=(t // tm,),
        in_specs=[
            pl.BlockSpec((tm, D_MODEL), row), pl.BlockSpec((tm, SSM_WIDTH), row),
            pl.BlockSpec((tm, Q_WIDTH), row),
            _const_spec((1, D_MODEL)), _const_spec(wg.shape),
            _const_spec(wbs.shape), _const_spec(wbn.shape), _const_spec(wo.shape),
        ],
        out_specs=pl.BlockSpec((tm, D_MODEL), row),
        out_shape=jax.ShapeDtypeStruct((t, D_MODEL), F32),
        scratch_shapes=[pltpu.VMEM((tm, D_MODEL), BF16)],
        compiler_params=_cparams("parallel"),
        name="merge",
    )(x2, z, a, g, wg, wbs, wbn, wo)


def _ffn_kernel(x_ref, xh_ref, g_ref, wa_ref, wb_ref, cw_ref, cb_ref, wo_ref, gf_ref, o_ref, a_s, act_s,
                *, tiles_per_seq, final):
    tm = x_ref.shape[0]
    halo = xh_ref.shape[0]
    first = (pl.program_id(0) % tiles_per_seq) == 0
    x = x_ref[...]
    g = g_ref[...]
    h = _rms(x, g).astype(BF16)
    hh = _rms(xh_ref[...], g).astype(BF16)
    keep_halo = jnp.where(first, 0.0, 1.0)
    n_chunks = D_FF // FF_CHUNK
    y = x
    for c in range(n_chunks):
        cs = slice(c * FF_CHUNK, (c + 1) * FF_CHUNK)
        a_buf = a_s.at[c % 2]
        a_buf[0:halo, :] = jnp.dot(hh, wa_ref[:, cs], preferred_element_type=F32) * keep_halo
        a_buf[halo:halo + tm, :] = jnp.dot(h, wa_ref[:, cs], preferred_element_type=F32)
        b = jnp.dot(h, wb_ref[:, cs], preferred_element_type=F32)
        cw = cw_ref[:, cs]
        conv = cb_ref[:, cs]
        a_all = a_buf[...]
        for k in range(CONV_WIDTH):
            sh = CONV_WIDTH - 1 - k
            a_k = a_all if sh == 0 else pltpu.roll(a_all, sh, 0)
            conv = conv + cw[k:k + 1, :] * a_k[halo:halo + tm, :]
        act_s[:, cs] = (_gelu(conv) * b).astype(BF16)
        if (c + 1) % FF_GROUP == 0 or c + 1 == n_chunks:
            ks = slice((c // FF_GROUP) * FF_GROUP * FF_CHUNK, (c + 1) * FF_CHUNK)
            y = y + jnp.dot(act_s[:, ks], wo_ref[ks, :], preferred_element_type=F32)
    if final:
        y = _rms(y, gf_ref[...])
    o_ref[...] = y


def _ffn(x2, g, wa, wb, cw, cb, wo, gf, seq, final):
    t = x2.shape[0]
    tm = ROW_TILE
    halo = SUBLANES
    tps = seq // tm
    row = lambda i: (i, 0)
    return pl.pallas_call(
        functools.partial(_ffn_kernel, tiles_per_seq=tps, final=final),
        grid=(t // tm,),
        in_specs=[
            pl.BlockSpec((tm, D_MODEL), row),
            pl.BlockSpec((halo, D_MODEL), lambda i: (jnp.maximum(i * (tm // halo) - 1, 0), 0)),
            _const_spec((1, D_MODEL)),
            _const_spec(wa.shape), _const_spec(wb.shape),
            _const_spec(cw.shape), _const_spec(cb.shape), _const_spec(wo.shape),
            _const_spec((1, D_MODEL)),
        ],
        out_specs=pl.BlockSpec((tm, D_MODEL), row),
        out_shape=jax.ShapeDtypeStruct((t, D_MODEL), F32),
        scratch_shapes=[pltpu.VMEM((2, tm + halo, FF_CHUNK), F32), pltpu.VMEM((tm, D_FF), BF16)],
        compiler_params=_cparams("parallel"),
        name="ffn",
    )(x2, x2, g, wa, wb, cw, cb, wo, gf)


def _mixer_layer(x2, bsz, seq, cos, sin, norm_g, w_in, s5p, pe, w1, b1, w2, wbs, wbn, wo, selmt):
    u, qt, kvc, ks_a, kw, vst, vwt, gatet = _inproj(x2, norm_g, w_in[:, :_C_GSSM], cos, sin, seq)

    z = _s5(u.reshape(bsz, seq, SSM_WIDTH), s5p).reshape(bsz * seq, SSM_WIDTH)

    kvc_c = _compress(kvc.reshape(bsz, seq, 2 * KV_WIDTH), pe, w1, b1, w2)
    kc = jnp.pad(kvc_c[:, :N_KV_HEADS].astype(BF16), ((0, 0),) * 3 + ((0, LANES - HEAD_DIM),))
    vct = jnp.pad(kvc_c[:, N_KV_HEADS:].transpose(0, 1, 3, 2).astype(BF16),
                  ((0, 0), (0, 0), (0, LANES - HEAD_DIM), (0, 0)))
    a = _nsa_t(qt, kc, vct, selmt, ks_a, vst, kw, vwt, gatet).reshape(bsz * seq, Q_WIDTH)

    return _merge(x2, z, a, norm_g, w_in[:, _C_GSSM:], wbs, wbn, wo)


def kernel(x, norm_mix, w_in, ssm_a_re, ssm_a_im, ssm_log_dt, ssm_b_re, ssm_b_im, ssm_c_re, ssm_c_im,
           ssm_d, ssm_w_glu, cmp_pe_k, cmp_w1_k, cmp_b1_k, cmp_w2_k, cmp_pe_v, cmp_w1_v, cmp_b1_v,
           cmp_w2_v, w_branch_ssm, w_branch_nsa, w_out, norm_ffn, w_ffn_in, ffn_conv_w, ffn_conv_b,
           w_ffn_out, norm_final):
    bsz, seq, _ = x.shape
    depth = norm_mix.shape[0]
    assert seq % ROW_TILE == 0 and ROW_TILE % Q_TILE == 0 and seq >= WINDOW + Q_TILE
    cos, sin = _rope_tables(seq)
    assert N_SEL <= seq // SEL_BLOCK <= LANES, "selection blocks are mapped onto one vreg of lanes"
    selmt = _sel_fold_matrix(seq // CMP_STRIDE, LANES).T
    half = CMP_STRIDE * HEAD_DIM
    x2 = x.reshape(bsz * seq, D_MODEL)
    for l in range(depth):
        s5p = _prep_s5(ssm_a_re[l], ssm_a_im[l], ssm_log_dt[l], ssm_b_re[l], ssm_b_im[l],
                       ssm_c_re[l], ssm_c_im[l], ssm_d[l], ssm_w_glu[l])
        pe = jnp.stack([cmp_pe_k[l], cmp_pe_v[l]]).reshape(2, 2, half)
        w1 = jnp.stack([cmp_w1_k[l], cmp_w1_v[l]]).astype(BF16)
        b1 = jnp.stack([cmp_b1_k[l], cmp_b1_v[l]]).reshape(2, 1, CMP_HIDDEN)
        w2 = jnp.stack([cmp_w2_k[l], cmp_w2_v[l]]).astype(BF16)
        x2 = _mixer_layer(x2, bsz, seq, cos, sin, norm_mix[l].reshape(1, -1), _prep_w_in(w_in[l]), s5p,
                          pe, w1, b1, w2, w_branch_ssm[l].astype(BF16), w_branch_nsa[l].astype(BF16),
                          w_out[l].astype(BF16), selmt)
        wf = w_ffn_in[l].astype(BF16)
        x2 = _ffn(x2, norm_ffn[l].reshape(1, -1), wf[:, :D_FF], wf[:, D_FF:], ffn_conv_w[l],
                  ffn_conv_b[l].reshape(1, -1), w_ffn_out[l].astype(BF16), norm_final.reshape(1, -1),
                  seq, final=(l == depth - 1))
    return x2.reshape(bsz, seq, D_MODEL)
```
